```python
import jax, jax.numpy as jnp
from jax import lax
import numpy as np

D_MODEL = 1024
BATCH = 16
SEQ = 2048
DEPTH = 1
DEC_BATCH = 128
DEC_SEQ = 1
PAST_LEN = 8192
PAGE_SIZE = 128

N_META = 16
ATTN_HEADS = 8
HEAD_DIM = 64
ATTN_WIDTH = ATTN_HEADS * HEAD_DIM
CONV_CH = D_MODEL - ATTN_WIDTH
CONV_WIDTH = 31
D_FF = 4 * D_MODEL
Q_BLOCK = 128
IN_WIDTH = 3 * ATTN_WIDTH + ATTN_HEADS + 2 * CONV_CH
DEEPNORM_ALPHA = (2 * DEPTH) ** 0.25
DEEPNORM_BETA = (8 * DEPTH) ** -0.25
FORGET_BIAS = 3.0
LN_EPS = 1e-5
NEG_INF = -1e30

kernel_name = 'hymba_fox_conformer_deepnorm_step'


def _layer_norm(x, g, b):
    x32 = x.astype(jnp.float32)
    mu = jnp.mean(x32, axis=-1, keepdims=True)
    var = jnp.mean(jnp.square(x32 - mu), axis=-1, keepdims=True)
    y = (x32 - mu) * lax.rsqrt(var + LN_EPS) * g.astype(jnp.float32) + b.astype(jnp.float32)
    return y.astype(x.dtype)


def _in_proj(x, w_in, b_f):
    bsz, t, _ = x.shape
    z = jnp.einsum('btd,de->bte', x, w_in)
    splits = [ATTN_WIDTH, 2 * ATTN_WIDTH, 3 * ATTN_WIDTH, 3 * ATTN_WIDTH + ATTN_HEADS,
              3 * ATTN_WIDTH + ATTN_HEADS + CONV_CH]
    q, k, v, f_logit, u_val, u_gate = jnp.split(z, splits, axis=-1)
    logf = jax.nn.log_sigmoid(f_logit.astype(jnp.float32) + b_f.astype(jnp.float32))
    glu = u_val * jax.nn.sigmoid(u_gate)
    shp = (bsz, t, ATTN_HEADS, HEAD_DIM)
    return q.reshape(shp), k.reshape(shp), v.reshape(shp), logf, glu


def _attend(q, c_q, q_pos, segments):
    scale = HEAD_DIM ** -0.5
    cq = jnp.swapaxes(c_q, 1, 2)[..., :, None]
    logits = []
    for k, _, c_k, k_pos in segments:
        s = jnp.einsum('bqhd,bkhd->bhqk', q, k).astype(jnp.float32) * scale
        s = s + (cq - jnp.swapaxes(c_k, 1, 2)[..., None, :])
        visible = k_pos[None, :] <= q_pos[:, None]
        logits.append(jnp.where(visible, s, NEG_INF))
    p = jax.nn.softmax(jnp.concatenate(logits, axis=-1), axis=-1)
    out = None
    offset = 0
    for k, v, _, _ in segments:
        n = k.shape[1]
        part = jnp.einsum('bhqk,bkhd->bqhd', p[..., offset:offset + n].astype(v.dtype), v)
        out = part if out is None else out + part
        offset += n
    return out


def _conv_module_tail(u_ext, conv_w, conv_b, g, b):
    y = lax.conv_general_dilated(u_ext, conv_w[:, None, :].astype(u_ext.dtype), window_strides=(1,),
                                 padding='VALID', dimension_numbers=('NWC', 'WIO', 'NWC'),
                                 feature_group_count=CONV_CH)
    return jax.nn.silu(_layer_norm(y + conv_b.astype(y.dtype), g, b))


def _merge_and_mlp(x, attn, conv, w_out, ln1_g, ln1_b, w_up, w_down, ln2_g, ln2_b):
    bsz, t, _ = x.shape
    mixed = jnp.concatenate([attn.reshape(bsz, t, ATTN_WIDTH), conv], axis=-1) @ w_out
    h = _layer_norm(DEEPNORM_ALPHA * x + mixed, ln1_g, ln1_b)
    m = jnp.square(jax.nn.relu(h @ w_up)) @ w_down
    return _layer_norm(DEEPNORM_ALPHA * h + m, ln2_g, ln2_b)


def setup_inputs(seed: int = 0) -> dict:
    key = jax.random.key(seed)
    ks = jax.random.split(key, 24)
    nrm = jax.random.normal
    n_pages = PAST_LEN // PAGE_SIZE
    n_pool = (5 * DEC_BATCH * n_pages) // 4
    x_prompt = nrm(ks[0], (BATCH, SEQ, D_MODEL), jnp.float32)
    x_sample = nrm(ks[1], (DEC_BATCH, DEC_SEQ, D_MODEL), jnp.float32)
    cache_k = nrm(ks[2], (DEPTH, n_pool, PAGE_SIZE, ATTN_HEADS, HEAD_DIM), jnp.float32)
    cache_v = nrm(ks[3], (DEPTH, n_pool, PAGE_SIZE, ATTN_HEADS, HEAD_DIM), jnp.float32) * DEEPNORM_BETA
    cache_logf = jax.nn.log_sigmoid(FORGET_BIAS + nrm(ks[4], (DEPTH, n_pool, PAGE_SIZE, ATTN_HEADS), jnp.float32))
    state_conv = 0.5 * nrm(ks[5], (DEPTH, DEC_BATCH, CONV_WIDTH - 1, CONV_CH), jnp.float32)
    page_table = jax.random.permutation(ks[6], n_pool)[:DEC_BATCH * n_pages].reshape(
        DEC_BATCH, n_pages).astype(jnp.int32)
    meta_tokens = nrm(ks[7], (N_META, D_MODEL), jnp.float32)
    w_in = nrm(ks[8], (DEPTH, D_MODEL, IN_WIDTH), jnp.float32) * D_MODEL ** -0.5
    w_in = w_in.at[:, :, 2 * ATTN_WIDTH:3 * ATTN_WIDTH].multiply(DEEPNORM_BETA)
    b_f = FORGET_BIAS + 0.1 * nrm(ks[9], (DEPTH, ATTN_HEADS), jnp.float32)
    conv_w = nrm(ks[10], (DEPTH, CONV_WIDTH, CONV_CH), jnp.float32) * CONV_WIDTH ** -0.5
    conv_b = 0.02 * nrm(ks[11], (DEPTH, CONV_CH), jnp.float32)
    conv_ln_g = 1.0 + 0.02 * nrm(ks[12], (DEPTH, CONV_CH), jnp.float32)
    conv_ln_b = 0.02 * nrm(ks[13], (DEPTH, CONV_CH), jnp.float32)
    w_out = nrm(ks[14], (DEPTH, D_MODEL, D_MODEL), jnp.float32) * D_MODEL ** -0.5 * DEEPNORM_BETA
    ln1_g = 1.0 + 0.02 * nrm(ks[15], (DEPTH, D_MODEL), jnp.float32)
    ln1_b = 0.02 * nrm(ks[16], (DEPTH, D_MODEL), jnp.float32)
    w_up = nrm(ks[17], (DEPTH, D_MODEL, D_FF), jnp.float32) * D_MODEL ** -0.5
    w_down = nrm(ks[18], (DEPTH, D_FF, D_MODEL), jnp.float32) * D_FF ** -0.5 * DEEPNORM_BETA
    ln2_g = 1.0 + 0.02 * nrm(ks[19], (DEPTH, D_MODEL), jnp.float32)
    ln2_b = 0.02 * nrm(ks[20], (DEPTH, D_MODEL), jnp.float32)
    return {'x_prompt': x_prompt, 'x_sample': x_sample, 'cache_k': cache_k, 'cache_v': cache_v,
            'cache_logf': cache_logf, 'state_conv': state_conv, 'page_table': page_table,
            'meta_tokens': meta_tokens, 'w_in': w_in, 'b_f': b_f, 'conv_w': conv_w, 'conv_b': conv_b,
            'conv_ln_g': conv_ln_g, 'conv_ln_b': conv_ln_b, 'w_out': w_out, 'ln1_g': ln1_g,
            'ln1_b': ln1_b, 'w_up': w_up, 'w_down': w_down, 'ln2_g': ln2_g, 'ln2_b': ln2_b}


def reference(x_prompt, x_sample, cache_k, cache_v, cache_logf, state_conv, page_table, meta_tokens,
              w_in, b_f, conv_w, conv_b, conv_ln_g, conv_ln_b, w_out, ln1_g, ln1_b, w_up, w_down,
              ln2_g, ln2_b):
    bsz, seq, _ = x_prompt.shape
    dbsz, dseq, _ = x_sample.shape
    p_len = page_table.shape[1] * PAGE_SIZE
    n_blocks = seq // Q_BLOCK
    tail = CONV_WIDTH - 1

    xp = jnp.concatenate([jnp.broadcast_to(meta_tokens[None].astype(x_prompt.dtype), (bsz, N_META, D_MODEL)),
                          x_prompt], axis=1)
    xs = x_sample
    pos = jnp.arange(N_META + seq)
    past_pos = jnp.arange(p_len)
    new_pos = p_len + jnp.arange(dseq)

    k_p, v_p, lf_p, cv_p, k_s, v_s, lf_s, cv_s = [], [], [], [], [], [], [], []
    for d in range(DEPTH):
        q, k, v, logf, glu = _in_proj(xp, w_in[d], b_f[d])
        c = jnp.cumsum(logf, axis=1)
        meta_out = _attend(q[:, :N_META], c[:, :N_META], pos[:N_META],
                           [(k[:, :N_META], v[:, :N_META], c[:, :N_META], pos[:N_META])])
        qb = jnp.swapaxes(q[:, N_META:].reshape(bsz, n_blocks, Q_BLOCK, ATTN_HEADS, HEAD_DIM), 0, 1)
        cb = jnp.swapaxes(c[:, N_META:].reshape(bsz, n_blocks, Q_BLOCK, ATTN_HEADS), 0, 1)
        pb = pos[N_META:].reshape(n_blocks, Q_BLOCK)
        blocks = lax.map(lambda a: _attend(a[0], a[1], a[2], [(k, v, c, pos)]), (qb, cb, pb))
        attn_p = jnp.concatenate(
            [meta_out, jnp.swapaxes(blocks, 0, 1).reshape(bsz, seq, ATTN_HEADS, HEAD_DIM)], axis=1)
        u_ext = jnp.pad(glu, ((0, 0), (tail, 0), (0, 0)))
        conv_p = _conv_module_tail(u_ext, conv_w[d], conv_b[d], conv_ln_g[d], conv_ln_b[d])
        xp = _merge_and_mlp(xp, attn_p, conv_p, w_out[d], ln1_g[d], ln1_b[d], w_up[d], w_down[d],
                            ln2_g[d], ln2_b[d])
        k_p.append(k)
        v_p.append(v)
        lf_p.append(logf)
        cv_p.append(glu[:, -tail:])

        qs, ks_, vs_, logfs, glus = _in_proj(xs, w_in[d], b_f[d])
        past_k = cache_k[d][page_table].reshape(dbsz, p_len, ATTN_HEADS, HEAD_DIM)
        past_v = cache_v[d][page_table].reshape(dbsz, p_len, ATTN_HEADS, HEAD_DIM)
        past_lf = cache_logf[d][page_table].reshape(dbsz, p_len, ATTN_HEADS).astype(jnp.float32)
        c_all = jnp.cumsum(jnp.concatenate([past_lf, logfs], axis=1), axis=1)
        c_past, c_new = c_all[:, :p_len], c_all[:, p_len:]
        attn_s = _attend(qs, c_new, new_pos,
                         [(past_k, past_v, c_past, past_pos), (ks_, vs_, c_new, new_pos)])
        u_ext_s = jnp.concatenate([state_conv[d].astype(glus.dtype), glus], axis=1)
        conv_s = _conv_module_tail(u_ext_s, conv_w[d], conv_b[d], conv_ln_g[d], conv_ln_b[d])
        xs = _merge_and_mlp(xs, attn_s, conv_s, w_out[d], ln1_g[d], ln1_b[d], w_up[d], w_down[d],
                            ln2_g[d], ln2_b[d])
        k_s.append(ks_)
        v_s.append(vs_)
        lf_s.append(logfs)
        cv_s.append(u_ext_s[:, -tail:])

    y_prompt = xp[:, N_META:]
    return (y_prompt, xs, jnp.stack(k_p), jnp.stack(v_p), jnp.stack(lf_p), jnp.stack(cv_p),
            jnp.stack(k_s), jnp.stack(v_s), jnp.stack(lf_s), jnp.stack(cv_s))
```

```python
import functools

import jax
import jax.numpy as jnp
from jax import lax
from jax.experimental import pallas as pl
from jax.experimental.pallas import tpu as pltpu

N_META = 16
HEADS = 8
HEAD_DIM = 64
ATTN_W = HEADS * HEAD_DIM
CONV_CH = 512
CONV_W = 31
TAIL = CONV_W - 1
PAGE = 128
LANES = 128
ALPHA = 2.0 ** 0.25
LN_EPS = 1e-5
NEG = -1e30
SCALE = HEAD_DIM ** -0.5

F32 = jnp.float32
BF16 = jnp.bfloat16

VMEM_LIMIT = 56 * 1024 * 1024


def _const_spec(shape):
    nd = len(shape)
    return pl.BlockSpec(shape, lambda *_: (0,) * nd, pipeline_mode=pl.Buffered(1))


def _split3(x):
    hi = x.astype(BF16)
    r1 = x - hi.astype(F32)
    mid = r1.astype(BF16)
    lo = (r1 - mid.astype(F32)).astype(BF16)
    return hi, mid, lo


def _dot_exact01(x, m01, *, lhs=True):
    out = None
    for piece in _split3(x):
        if lhs:
            t = jnp.dot(piece, m01, preferred_element_type=F32)
        else:
            t = jnp.dot(m01, piece, preferred_element_type=F32)
        out = t if out is None else out + t
    return out


def _layer_norm(x, g, b):
    mu = jnp.mean(x, axis=-1, keepdims=True)
    xc = x - mu
    var = jnp.mean(xc * xc, axis=-1, keepdims=True)
    return xc * lax.rsqrt(var + LN_EPS) * g + b


def _in_proj_kernel(x_ref, wqkv_ref, wf_ref, wu_ref, wg_ref, bf_ref,
                    q_ref, k_ref, v_ref, kb_ref, vb_ref, lf_ref, glu_ref):
    x = x_ref[...].astype(BF16)
    z = jnp.dot(x, wqkv_ref[...], preferred_element_type=F32)
    q_ref[...] = (z[:, :ATTN_W] * SCALE).astype(BF16)
    k = z[:, ATTN_W:2 * ATTN_W]
    v = z[:, 2 * ATTN_W:]
    k_ref[...] = k
    v_ref[...] = v
    kb_ref[...] = k.astype(BF16)
    vb_ref[...] = v.astype(BF16)
    f = jnp.dot(x, wf_ref[...], preferred_element_type=F32) + bf_ref[...]
    lf_ref[...] = jnp.minimum(f, 0.0) - jnp.log(1.0 + jnp.exp(-jnp.abs(f)))
    u = jnp.dot(x, wu_ref[...], preferred_element_type=F32)
    g = jnp.dot(x, wg_ref[...], preferred_element_type=F32)
    glu_ref[...] = u * (1.0 / (1.0 + jnp.exp(-g)))


def _in_proj(x, wqkv, wf, wu, wg, bf, bm):
    m, d = x.shape
    assert m % bm == 0
    row = lambda w: pl.BlockSpec((bm, w), lambda i: (i, 0))
    outs = [
        jax.ShapeDtypeStruct((m, ATTN_W), BF16),
        jax.ShapeDtypeStruct((m, ATTN_W), F32),
        jax.ShapeDtypeStruct((m, ATTN_W), F32),
        jax.ShapeDtypeStruct((m, ATTN_W), BF16),
        jax.ShapeDtypeStruct((m, ATTN_W), BF16),
        jax.ShapeDtypeStruct((m, LANES), F32),
        jax.ShapeDtypeStruct((m, CONV_CH), F32),
    ]
    return pl.pallas_call(
        _in_proj_kernel,
        grid=(m // bm,),
        in_specs=[row(d), _const_spec(wqkv.shape), _const_spec(wf.shape), _const_spec(wu.shape),
                  _const_spec(wg.shape), _const_spec(bf.shape)],
        out_specs=[row(ATTN_W), row(ATTN_W), row(ATTN_W), row(ATTN_W), row(ATTN_W), row(LANES),
                   row(CONV_CH)],
        out_shape=outs,
        compiler_params=pltpu.CompilerParams(dimension_semantics=("arbitrary",),
                                             vmem_limit_bytes=VMEM_LIMIT),
        name="in_proj",
    )(x, wqkv, wf, wu, wg, bf)


CS_CHUNK = 256


def _cumsum_kernel(lfm_ref, lf_ref, cm_ref, cmt_ref, c_ref, ct_ref):
    r16 = lax.broadcasted_iota(jnp.int32, (N_META, N_META), 0)
    c16 = lax.broadcasted_iota(jnp.int32, (N_META, N_META), 1)
    tri16 = (c16 <= r16).astype(BF16)
    cm = _dot_exact01(lfm_ref[...], tri16, lhs=False)
    cm_ref[...] = cm
    cmt_ref[...] = jnp.transpose(jnp.concatenate([cm, jnp.zeros((LANES - N_META, LANES), F32)], axis=0))[:HEADS]
    carry = cm[N_META - 1:N_META, :]
    rr = lax.broadcasted_iota(jnp.int32, (CS_CHUNK, CS_CHUNK), 0)
    cc = lax.broadcasted_iota(jnp.int32, (CS_CHUNK, CS_CHUNK), 1)
    tri = (cc <= rr).astype(BF16)
    t = lf_ref.shape[1]
    for i in range(t // CS_CHUNK):
        sl = pl.ds(i * CS_CHUNK, CS_CHUNK)
        c = _dot_exact01(lf_ref[0, sl, :], tri, lhs=False) + carry
        c_ref[0, sl, :] = c
        ct_ref[0, :, sl] = jnp.transpose(c)[:HEADS]
        carry = c[CS_CHUNK - 1:CS_CHUNK, :]


def _prompt_cumsum(lf_meta, lf):
    b, t, _ = lf.shape
    return pl.pallas_call(
        _cumsum_kernel,
        grid=(b,),
        in_specs=[_const_spec((N_META, LANES)), pl.BlockSpec((1, t, LANES), lambda i: (i, 0, 0))],
        out_specs=[pl.BlockSpec((N_META, LANES), lambda i: (0, 0)),
                   pl.BlockSpec((HEADS, LANES), lambda i: (0, 0)),
                   pl.BlockSpec((1, t, LANES), lambda i: (i, 0, 0)),
                   pl.BlockSpec((1, HEADS, t), lambda i: (i, 0, 0))],
        out_shape=[jax.ShapeDtypeStruct((N_META, LANES), F32),
                   jax.ShapeDtypeStruct((HEADS, LANES), F32),
                   jax.ShapeDtypeStruct((b, t, LANES), F32),
                   jax.ShapeDtypeStruct((b, HEADS, t), F32)],
        compiler_params=pltpu.CompilerParams(dimension_semantics=("arbitrary",),
                                             vmem_limit_bytes=VMEM_LIMIT),
        name="prompt_cumsum",
    )(lf_meta, lf)


TQ = 256
TK = 256


def _flash_kernel(q_ref, kb_ref, vb_ref, c_ref, ct_ref, km_ref, vm_ref, cmt_ref, o_ref):
    hp = pl.program_id(1)
    qi = pl.program_id(2)
    q = q_ref[0]
    lane = lax.broadcasted_iota(jnp.int32, (TQ, LANES), 1)
    c_blk = c_ref[0]
    row = lax.broadcasted_iota(jnp.int32, (TQ, TK), 0)
    col = lax.broadcasted_iota(jnp.int32, (TQ, TK), 1)
    outs = []
    for hh in range(2):
        h = 2 * hp + hh
        in_half = (lane // HEAD_DIM) == hh
        qh = jnp.where(in_half, q, jnp.zeros_like(q))
        cq = jnp.sum(jnp.where(lane == h, c_blk, 0.0), axis=1, keepdims=True)

        s = lax.dot_general(qh, km_ref[...], (((1,), (1,)), ((), ())), preferred_element_type=F32)
        s = s + (cq - cmt_ref[pl.ds(h, 1), pl.ds(0, N_META)])
        m = jnp.max(s, axis=1, keepdims=True)
        p = jnp.exp(s - m)
        l = jnp.sum(p, axis=1, keepdims=True)
        acc = jnp.dot(p.astype(BF16), vm_ref[...], preferred_element_type=F32)

        def step(j, carry, masked):
            m, l, acc = carry
            off = pl.multiple_of(j * TK, TK)
            kblk = kb_ref[0, pl.ds(off, TK), :]
            vblk = vb_ref[0, pl.ds(off, TK), :]
            s = lax.dot_general(qh, kblk, (((1,), (1,)), ((), ())), preferred_element_type=F32)
            s = s + (cq - ct_ref[0, pl.ds(h, 1), pl.ds(off, TK)])
            if masked:
                s = jnp.where(col <= row, s, NEG)
            m_new = jnp.maximum(m, jnp.max(s, axis=1, keepdims=True))
            alpha = jnp.exp(m - m_new)
            p = jnp.exp(s - m_new)
            l = alpha * l + jnp.sum(p, axis=1, keepdims=True)
            acc = alpha * acc + jnp.dot(p.astype(BF16), vblk, preferred_element_type=F32)
            return m_new, l, acc

        m, l, acc = lax.fori_loop(0, qi, functools.partial(step, masked=False), (m, l, acc))
        m, l, acc = step(qi, (m, l, acc), True)
        outs.append(acc / l)
    o_ref[0] = jnp.where(lane < HEAD_DIM, outs[0], outs[1]).astype(o_ref.dtype)


def _prompt_attention(q, kb, vb, c, ct, kmb, vmb, cmt):
    b, t, _ = q.shape
    assert t % TQ == 0 and TQ == TK
    grid = (b, ATTN_W // LANES, t // TQ)
    return pl.pallas_call(
        _flash_kernel,
        grid=grid,
        in_specs=[
            pl.BlockSpec((1, TQ, LANES), lambda i, p, j: (i, j, p)),
            pl.BlockSpec((1, t, LANES), lambda i, p, j: (i, 0, p)),
            pl.BlockSpec((1, t, LANES), lambda i, p, j: (i, 0, p)),
            pl.BlockSpec((1, TQ, LANES), lambda i, p, j: (i, j, 0)),
            pl.BlockSpec((1, HEADS, t), lambda i, p, j: (i, 0, 0)),
            pl.BlockSpec((N_META, LANES), lambda i, p, j: (0, p)),
            pl.BlockSpec((N_META, LANES), lambda i, p, j: (0, p)),
            pl.BlockSpec((HEADS, LANES), lambda i, p, j: (0, 0)),
        ],
        out_specs=pl.BlockSpec((1, TQ, LANES), lambda i, p, j: (i, j, p)),
        out_shape=jax.ShapeDtypeStruct((b, t, ATTN_W), BF16),
        compiler_params=pltpu.CompilerParams(dimension_semantics=("arbitrary",) * 3,
                                             vmem_limit_bytes=VMEM_LIMIT),
        name="prompt_attention",
    )(q, kb, vb, c, ct, kmb, vmb, cmt)


CONV_BT = 256
CONV_SUB = 32
CONV_HALO = 32


def _conv_kernel(prev_ref, cur_ref, meta_ref, w_ref, cb_ref, g_ref, b_ref, o_ref, win_ref):
    i = pl.program_id(1)
    win_ref[pl.ds(0, CONV_HALO), :] = jnp.where(i == 0, meta_ref[...], prev_ref[0])
    win_ref[pl.ds(CONV_HALO, CONV_BT), :] = cur_ref[0]
    w = w_ref[...]
    base = CONV_HALO - TAIL
    for s0 in range(0, CONV_BT, CONV_SUB):
        acc = jnp.zeros((CONV_SUB, CONV_CH), F32)
        for j in range(CONV_W):
            acc = acc + w[j:j + 1, :] * win_ref[pl.ds(s0 + base + j, CONV_SUB), :]
        y = _layer_norm(acc + cb_ref[...], g_ref[...], b_ref[...])
        o_ref[0, pl.ds(s0, CONV_SUB), :] = (y * (1.0 / (1.0 + jnp.exp(-y)))).astype(o_ref.dtype)


def _prompt_conv(glu, meta_halo, conv_w, conv_b, g, b):
    bsz, t, _ = glu.shape
    assert t % CONV_BT == 0
    r = CONV_BT // CONV_HALO
    return pl.pallas_call(
        _conv_kernel,
        grid=(bsz, t // CONV_BT),
        in_specs=[
            pl.BlockSpec((1, CONV_HALO, CONV_CH), lambda i, j: (i, jnp.maximum(j * r - 1, 0), 0)),
            pl.BlockSpec((1, CONV_BT, CONV_CH), lambda i, j: (i, j, 0)),
            _const_spec(meta_halo.shape), _const_spec(conv_w.shape), _const_spec(conv_b.shape),
            _const_spec(g.shape), _const_spec(b.shape),
        ],
        out_specs=pl.BlockSpec((1, CONV_BT, CONV_CH), lambda i, j: (i, j, 0)),
        out_shape=jax.ShapeDtypeStruct((bsz, t, CONV_CH), BF16),
        scratch_shapes=[pltpu.VMEM((CONV_HALO + CONV_BT, CONV_CH), F32)],
        compiler_params=pltpu.CompilerParams(dimension_semantics=("arbitrary",) * 2,
                                             vmem_limit_bytes=VMEM_LIMIT),
        name="prompt_conv",
    )(glu, glu, meta_halo, conv_w, conv_b, g, b)


def _sample_conv_kernel(st_ref, glu_ref, w_ref, cb_ref, g_ref, b_ref, o_ref):
    w = w_ref[...]
    acc = w[TAIL:TAIL + 1, :] * glu_ref[...]
    for j in range(TAIL):
        acc = acc + w[j:j + 1, :] * st_ref[j]
    y = _layer_norm(acc + cb_ref[...], g_ref[...], b_ref[...])
    o_ref[...] = (y * (1.0 / (1.0 + jnp.exp(-y)))).astype(o_ref.dtype)


def _sample_conv(state_t, glus, conv_w, conv_b, g, b):
    n = glus.shape[0]
    return pl.pallas_call(
        _sample_conv_kernel,
        out_shape=jax.ShapeDtypeStruct((n, CONV_CH), BF16),
        compiler_params=pltpu.CompilerParams(vmem_limit_bytes=VMEM_LIMIT),
        name="sample_conv",
    )(state_t, glus, conv_w, conv_b, g, b)


FF_CHUNK = 1024


def _mlp_kernel(x_ref, a_ref, c_ref, woa_ref, woc_ref, g1_ref, b1_ref, wup_ref, wdn_ref, g2_ref, b2_ref,
                o_ref):
    mixed = jnp.dot(a_ref[...], woa_ref[...], preferred_element_type=F32)
    mixed = mixed + jnp.dot(c_ref[...], woc_ref[...], preferred_element_type=F32)
    h = _layer_norm(ALPHA * x_ref[...] + mixed, g1_ref[...], b1_ref[...])
    hb = h.astype(BF16)
    d_ff = wup_ref.shape[1]
    m = None
    for c0 in range(0, d_ff, FF_CHUNK):
        u = jnp.dot(hb, wup_ref[:, pl.ds(c0, FF_CHUNK)], preferred_element_type=F32)
        u = jnp.maximum(u, 0.0)
        u = (u * u).astype(BF16)
        t = jnp.dot(u, wdn_ref[pl.ds(c0, FF_CHUNK), :], preferred_element_type=F32)
        m = t if m is None else m + t
    o_ref[...] = _layer_norm(ALPHA * h + m, g2_ref[...], b2_ref[...])


def _merge_mlp(x, attn, conv, woa, woc, g1, b1, wup, wdn, g2, b2, bm):
    m, d = x.shape
    assert m % bm == 0
    row = lambda w: pl.BlockSpec((bm, w), lambda i: (i, 0))
    consts = [woa, woc, g1, b1, wup, wdn, g2, b2]
    return pl.pallas_call(
        _mlp_kernel,
        grid=(m // bm,),
        in_specs=[row(d), row(ATTN_W), row(CONV_CH)] + [_const_spec(a.shape) for a in consts],
        out_specs=row(d),
        out_shape=jax.ShapeDtypeStruct((m, d), F32),
        compiler_params=pltpu.CompilerParams(dimension_semantics=("arbitrary",),
                                             vmem_limit_bytes=VMEM_LIMIT),
        name="merge_mlp",
    )(x, attn, conv, *consts)


def _paged_bias_kernel(pt_ref, lfn_ref, lft_hbm, d_ref, buf, sem):
    b = pl.program_id(0)
    nb = pl.num_programs(0)
    n_pages = buf.shape[1]

    def copy(seq, slot, p):
        return pltpu.make_async_copy(lft_hbm.at[pt_ref[seq * n_pages + p]], buf.at[slot, p], sem.at[slot])

    def start_all(seq, slot):
        def body(p, _):
            copy(seq, slot, p).start()
            return 0
        lax.fori_loop(0, n_pages, body, 0)

    @pl.when(b == 0)
    def _():
        start_all(0, 0)

    @pl.when(b + 1 < nb)
    def _():
        start_all(b + 1, (b + 1) % 2)

    slot = b % 2

    def wait_body(p, _):
        copy(b, slot, p).wait()
        return 0
    lax.fori_loop(0, n_pages, wait_body, 0)

    rows = n_pages * HEADS
    lf = buf[slot].reshape(rows, PAGE)
    s_in = lax.broadcasted_iota(jnp.int32, (PAGE, PAGE), 0)
    s_out = lax.broadcasted_iota(jnp.int32, (PAGE, PAGE), 1)
    later = (s_in > s_out).astype(BF16)
    ones = jnp.ones((PAGE, PAGE), BF16)
    local = _dot_exact01(lf, later, lhs=True)
    tot = _dot_exact01(lf, ones, lhs=True)
    r_out = lax.broadcasted_iota(jnp.int32, (rows, rows), 0)
    r_in = lax.broadcasted_iota(jnp.int32, (rows, rows), 1)
    later_pages = ((r_in % HEADS == r_out % HEADS) & (r_in // HEADS > r_out // HEADS)).astype(BF16)
    cross = _dot_exact01(tot, later_pages, lhs=False)
    d = (local + cross).reshape(n_pages, HEADS, PAGE) + lfn_ref[0][None]
    d_ref[0] = d.reshape(rows, PAGE)


def _paged_bias(pt_flat, lfn_rep, lft_pool, n_pages):
    n = lfn_rep.shape[0]
    rows = n_pages * HEADS
    grid_spec = pltpu.PrefetchScalarGridSpec(
        num_scalar_prefetch=1,
        grid=(n,),
        in_specs=[pl.BlockSpec((1, HEADS, PAGE), lambda i, pt: (i, 0, 0)),
                  pl.BlockSpec(memory_space=pl.ANY)],
        out_specs=pl.BlockSpec((1, rows, PAGE), lambda i, pt: (i, 0, 0)),
        scratch_shapes=[pltpu.VMEM((2, n_pages, HEADS, PAGE), F32), pltpu.SemaphoreType.DMA((2,))],
    )
    return pl.pallas_call(
        _paged_bias_kernel,
        grid_spec=grid_spec,
        out_shape=jax.ShapeDtypeStruct((n, rows, PAGE), F32),
        compiler_params=pltpu.CompilerParams(dimension_semantics=("arbitrary",),
                                             vmem_limit_bytes=VMEM_LIMIT),
        name="paged_bias",
    )(pt_flat, lfn_rep, lft_pool)


NBUF = 16
PGROUP = 4
QROWS = 16


def _paged_attn_kernel(pt_ref, q_ref, kn_ref, vn_ref, d_ref, ck_hbm, cv_hbm, o_ref,
                       buf, sem, s_all):
    b = pl.program_id(0)
    nb = pl.num_programs(0)
    n_pages = s_all.shape[0]
    units = 2 * n_pages
    total = nb * units

    def start(unit_global):
        seq = unit_global // units
        u = unit_global % units
        slot = unit_global % NBUF
        page = pt_ref[seq * n_pages + u % n_pages]

        @pl.when(u < n_pages)
        def _():
            pltpu.make_async_copy(ck_hbm.at[page], buf.at[slot], sem.at[slot]).start()

        @pl.when(u >= n_pages)
        def _():
            pltpu.make_async_copy(cv_hbm.at[page], buf.at[slot], sem.at[slot]).start()

    def wait(slot):
        pltpu.make_async_copy(ck_hbm.at[0], buf.at[slot], sem.at[slot]).wait()

    @pl.when(b == 0)
    def _():
        for i in range(NBUF):
            start(i)

    base = b * units

    head_row = lax.broadcasted_iota(jnp.int32, (QROWS, ATTN_W), 0)
    head_col = lax.broadcasted_iota(jnp.int32, (QROWS, ATTN_W), 1) // HEAD_DIM
    own = head_row == head_col
    q = q_ref[0].astype(F32)
    wq32 = jnp.where(own, jnp.broadcast_to(q, (QROWS, ATTN_W)), 0.0)
    wq = wq32.astype(BF16)
    kn = kn_ref[0].astype(BF16).astype(F32)
    vn = vn_ref[0].astype(BF16).astype(F32)
    s_new = jnp.sum(wq32 * kn, axis=1, keepdims=True)[:HEADS]

    def key_group(g, _):
        for i in range(PGROUP):
            u = g * PGROUP + i
            slot = (base + u) % NBUF
            wait(slot)
            kp = buf[slot].astype(BF16)
            s = lax.dot_general(wq, kp, (((1,), (1,)), ((), ())), preferred_element_type=F32)
            s_all[u] = s[:HEADS] + d_ref[0, pl.ds(pl.multiple_of(u * HEADS, HEADS), HEADS), :]

            @pl.when(base + u + NBUF < total)
            def _():
                start(base + u + NBUF)
        return 0

    lax.fori_loop(0, n_pages // PGROUP, key_group, 0)

    s = s_all[...]
    m = jnp.max(jnp.max(s, axis=0), axis=1, keepdims=True)
    m = jnp.maximum(m, s_new)
    p = jnp.exp(s - m[None])
    l = jnp.sum(jnp.sum(p, axis=0), axis=1, keepdims=True)
    p_new = jnp.exp(s_new - m)
    l = l + p_new
    s_all[...] = p

    acc0 = jnp.concatenate([p_new, jnp.zeros((QROWS - HEADS, 1), F32)], axis=0) * vn

    def value_group(g, acc):
        for i in range(PGROUP):
            u = g * PGROUP + i
            slot = (base + n_pages + u) % NBUF
            wait(slot)
            vp = buf[slot].astype(BF16)
            pu = jnp.concatenate([s_all[u], jnp.zeros((QROWS - HEADS, PAGE), F32)], axis=0).astype(BF16)
            acc = acc + jnp.dot(pu, vp, preferred_element_type=F32)

            @pl.when(base + n_pages + u + NBUF < total)
            def _():
                start(base + n_pages + u + NBUF)
        return acc

    acc = lax.fori_loop(0, n_pages // PGROUP, value_group, acc0)
    l16 = jnp.concatenate([l, jnp.ones((QROWS - HEADS, 1), F32)], axis=0)
    out = jnp.sum(jnp.where(own, acc / l16, 0.0), axis=0, keepdims=True)
    o_ref[0] = out.astype(o_ref.dtype)


def _paged_attention(pt_flat, q, kn, vn, d_all, cache_k, cache_v, n_pages):
    n = q.shape[0]
    assert n_pages % PGROUP == 0 and (2 * n_pages) % NBUF == 0
    rows = n_pages * HEADS
    vec = pl.BlockSpec((1, 1, ATTN_W), lambda i, pt: (i, 0, 0))
    grid_spec = pltpu.PrefetchScalarGridSpec(
        num_scalar_prefetch=1,
        grid=(n,),
        in_specs=[vec, vec, vec,
                  pl.BlockSpec((1, rows, PAGE), lambda i, pt: (i, 0, 0)),
                  pl.BlockSpec(memory_space=pl.ANY), pl.BlockSpec(memory_space=pl.ANY)],
        out_specs=vec,
        scratch_shapes=[pltpu.VMEM((NBUF, PAGE, ATTN_W), F32), pltpu.SemaphoreType.DMA((NBUF,)),
                        pltpu.VMEM((n_pages, HEADS, PAGE), F32)],
    )
    return pl.pallas_call(
        _paged_attn_kernel,
        grid_spec=grid_spec,
        out_shape=jax.ShapeDtypeStruct((n, 1, ATTN_W), BF16),
        compiler_params=pltpu.CompilerParams(dimension_semantics=("arbitrary",),
                                             vmem_limit_bytes=VMEM_LIMIT),
        name="paged_attention",
    )(pt_flat, q, kn, vn, d_all, cache_k, cache_v)


def _pick_bm(m, target):
    bm = min(m, target)
    while m % bm:
        bm //= 2
    return bm


def kernel(x_prompt, x_sample, cache_k, cache_v, cache_logf, state_conv, page_table, meta_tokens, w_in, b_f,
           conv_w, conv_b, conv_ln_g, conv_ln_b, w_out, ln1_g, ln1_b, w_up, w_down, ln2_g, ln2_b):
    bsz, seq, d = x_prompt.shape
    dbsz, dseq, _ = x_sample.shape
    depth = w_in.shape[0]
    assert depth == 1 and dseq == 1
    n_pool = cache_k.shape[1]
    n_pages = page_table.shape[1]

    w = w_in[0]
    o_f = 3 * ATTN_W
    wqkv = w[:, :o_f].astype(BF16)
    wf = jnp.pad(w[:, o_f:o_f + HEADS], ((0, 0), (0, LANES - HEADS))).astype(BF16)
    wu = w[:, o_f + HEADS:o_f + HEADS + CONV_CH].astype(BF16)
    wg = w[:, o_f + HEADS + CONV_CH:].astype(BF16)
    bf = jnp.pad(b_f[0], (0, LANES - HEADS)).reshape(1, LANES)
    woa = w_out[0, :ATTN_W].astype(BF16)
    woc = w_out[0, ATTN_W:].astype(BF16)
    wup = w_up[0].astype(BF16)
    wdn = w_down[0].astype(BF16)
    row = lambda a: a[0].reshape(1, -1)
    cw, cb, cg, cbeta = conv_w[0], row(conv_b), row(conv_ln_g), row(conv_ln_b)
    g1, b1, g2, b2 = row(ln1_g), row(ln1_b), row(ln2_g), row(ln2_b)

    xp = x_prompt.reshape(bsz * seq, d)
    q, k, v, kb, vb, lf, glu = _in_proj(xp, wqkv, wf, wu, wg, bf, _pick_bm(bsz * seq, 512))
    qm, km, vm, kmb, vmb, lfm, glum = _in_proj(meta_tokens, wqkv, wf, wu, wg, bf, N_META)
    del qm
    cm, cmt, c, ct = _prompt_cumsum(lfm, lf.reshape(bsz, seq, LANES))
    del cm
    attn = _prompt_attention(q.reshape(bsz, seq, ATTN_W), kb.reshape(bsz, seq, ATTN_W),
                             vb.reshape(bsz, seq, ATTN_W), c, ct, kmb, vmb, cmt)
    meta_halo = jnp.concatenate([jnp.zeros((CONV_HALO - N_META, CONV_CH), F32), glum], axis=0)
    glu3 = glu.reshape(bsz, seq, CONV_CH)
    conv = _prompt_conv(glu3, meta_halo, cw, cb, cg, cbeta)
    y_prompt = _merge_mlp(xp, attn.reshape(bsz * seq, ATTN_W), conv.reshape(bsz * seq, CONV_CH),
                          woa, woc, g1, b1, wup, wdn, g2, b2, _pick_bm(bsz * seq, 512))
    y_prompt = y_prompt.reshape(bsz, seq, d)

    def with_meta(meta, body, width):
        meta_b = jnp.broadcast_to(meta[None, :, :width], (bsz, N_META, width))
        return jnp.concatenate([meta_b, body.reshape(bsz, seq, -1)[:, :, :width]], axis=1)

    k_prompt = with_meta(km, k, ATTN_W).reshape(1, bsz, N_META + seq, HEADS, HEAD_DIM)
    v_prompt = with_meta(vm, v, ATTN_W).reshape(1, bsz, N_META + seq, HEADS, HEAD_DIM)
    logf_prompt = with_meta(lfm, lf, HEADS).reshape(1, bsz, N_META + seq, HEADS)
    conv_prompt = glu3[:, seq - TAIL:, :].reshape(1, bsz, TAIL, CONV_CH)

    xs = x_sample.reshape(dbsz, d)
    qs, ks, vs, _, _, lfs, glus = _in_proj(xs, wqkv, wf, wu, wg, bf, dbsz)
    pt_flat = page_table.reshape(-1)
    lft_pool = jnp.transpose(cache_logf[0], (0, 2, 1))
    lfn_rep = jnp.broadcast_to(lfs[:, :HEADS, None], (dbsz, HEADS, PAGE))
    d_all = _paged_bias(pt_flat, lfn_rep, lft_pool, n_pages)
    attn_s = _paged_attention(pt_flat, qs.reshape(dbsz, 1, ATTN_W), ks.reshape(dbsz, 1, ATTN_W),
                              vs.reshape(dbsz, 1, ATTN_W), d_all,
                              cache_k[0].reshape(n_pool, PAGE, ATTN_W),
                              cache_v[0].reshape(n_pool, PAGE, ATTN_W), n_pages)
    state_t = jnp.transpose(state_conv[0], (1, 0, 2))
    conv_s = _sample_conv(state_t, glus, cw, cb, cg, cbeta)
    y_sample = _merge_mlp(xs, attn_s.reshape(dbsz, ATTN_W), conv_s, woa, woc, g1, b1, wup, wdn, g2, b2, dbsz)
    y_sample = y_sample.reshape(dbsz, 1, d)

    k_sample = ks.reshape(1, dbsz, 1, HEADS, HEAD_DIM)
    v_sample = vs.reshape(1, dbsz, 1, HEADS, HEAD_DIM)
    logf_sample = lfs[:, :HEADS].reshape(1, dbsz, 1, HEADS)
    conv_sample = jnp.concatenate([state_conv[0][:, 1:, :], glus[:, None, :]], axis=1)[None]

    return (y_prompt, y_sample, k_prompt, v_prompt, logf_prompt, conv_prompt,
            k_sample, v_sample, logf_sample, conv_sample)
```

```python
import functools
import math

import jax
import jax.numpy as jnp
from jax import lax
from jax.experimental import pallas as pl
from jax.experimental.pallas import tpu as pltpu

N_META = 16
HEADS = 8
HEAD_DIM = 64
ATTN_W = HEADS * HEAD_DIM
CONV_CH = 512
CONV_W = 31
TAIL = CONV_W - 1
PAGE = 128
LANES = 128
BF16_ROWS = 16
ALPHA = 2.0 ** 0.25
LN_EPS = 1e-5
NEG = -1e30
LOG2E = math.log2(math.e)
QSCALE = HEAD_DIM ** -0.5 * LOG2E

F32 = jnp.float32
BF16 = jnp.bfloat16

VMEM_LIMIT = 56 * 1024 * 1024


def _const_spec(shape):
    nd = len(shape)
    return pl.BlockSpec(shape, lambda *_: (0,) * nd, pipeline_mode=pl.Buffered(1))


def _split3(x):
    hi = x.astype(BF16)
    r1 = x - hi.astype(F32)
    mid = r1.astype(BF16)
    lo = (r1 - mid.astype(F32)).astype(BF16)
    return hi, mid, lo


def _dot_exact01(x, m01, *, lhs=True):
    out = None
    for piece in _split3(x):
        if lhs:
            t = jnp.dot(piece, m01, preferred_element_type=F32)
        else:
            t = jnp.dot(m01, piece, preferred_element_type=F32)
        out = t if out is None else out + t
    return out


def _layer_norm(x, g, b):
    mu = jnp.mean(x, axis=-1, keepdims=True)
    xc = x - mu
    var = jnp.mean(xc * xc, axis=-1, keepdims=True)
    return xc * lax.rsqrt(var + LN_EPS) * g + b


V_ROWS = HEAD_DIM + BF16_ROWS


def _in_proj_kernel(x_ref, wqkv_ref, wf_ref, wu_ref, wg_ref, bf_ref, *out_refs, sample):
    x = x_ref[0].astype(BF16)
    z = jnp.dot(x, wqkv_ref[...], preferred_element_type=F32)
    q = z[:, :ATTN_W] * QSCALE
    kt = z[:, ATTN_W:2 * ATTN_W].T
    vt = z[:, 2 * ATTN_W:].T
    if sample:
        qt_ref, kt_ref, vt_ref, lf_ref, glu_ref = out_refs
        qt_ref[0] = q.T
    else:
        q_ref, kb_ref, kt_ref, vt_ref, vta_ref, lf_ref, glu_ref = out_refs
        q_ref[0] = q.astype(BF16)
        kb_ref[0] = z[:, ATTN_W:2 * ATTN_W].astype(BF16)
        ones = jnp.ones((BF16_ROWS, vt.shape[1]), BF16)
        for h in range(HEADS):
            vta_ref[0, pl.ds(h * V_ROWS, HEAD_DIM), :] = vt[h * HEAD_DIM:(h + 1) * HEAD_DIM].astype(BF16)
            vta_ref[0, pl.ds(h * V_ROWS + HEAD_DIM, BF16_ROWS), :] = ones
    kt_ref[0] = kt
    vt_ref[0] = vt
    f = jnp.dot(x, wf_ref[...], preferred_element_type=F32) + bf_ref[...]
    lf_ref[0] = jnp.minimum(f, 0.0) - jnp.log(1.0 + jnp.exp(-jnp.abs(f)))
    u = jnp.dot(x, wu_ref[...], preferred_element_type=F32)
    g = jnp.dot(x, wg_ref[...], preferred_element_type=F32)
    glu_ref[0] = u * (1.0 / (1.0 + jnp.exp(-g)))


def _in_proj(x, wqkv, wf, wu, wg, bf, bm, sample):
    b, t, d = x.shape
    assert t % bm == 0
    row = lambda w: pl.BlockSpec((1, bm, w), lambda i, j: (i, j, 0))
    col = lambda r: pl.BlockSpec((1, r, bm), lambda i, j: (i, 0, j))
    nat = lambda w, dt: jax.ShapeDtypeStruct((b, t, w), dt)
    tr = lambda r, dt: jax.ShapeDtypeStruct((b, r, t), dt)
    if sample:
        outs = [tr(ATTN_W, F32), tr(ATTN_W, F32), tr(ATTN_W, F32), nat(LANES, F32), nat(CONV_CH, F32)]
        specs = [col(ATTN_W), col(ATTN_W), col(ATTN_W), row(LANES), row(CONV_CH)]
    else:
        outs = [nat(ATTN_W, BF16), nat(ATTN_W, BF16), tr(ATTN_W, F32), tr(ATTN_W, F32),
                tr(HEADS * V_ROWS, BF16), nat(LANES, F32), nat(CONV_CH, F32)]
        specs = [row(ATTN_W), row(ATTN_W), col(ATTN_W), col(ATTN_W), col(HEADS * V_ROWS), row(LANES),
                 row(CONV_CH)]
    return pl.pallas_call(
        functools.partial(_in_proj_kernel, sample=sample),
        grid=(b, t // bm),
        in_specs=[row(d), _const_spec(wqkv.shape), _const_spec(wf.shape), _const_spec(wu.shape),
                  _const_spec(wg.shape), _const_spec(bf.shape)],
        out_specs=specs,
        out_shape=outs,
        compiler_params=pltpu.CompilerParams(dimension_semantics=("arbitrary", "arbitrary"),
                                             vmem_limit_bytes=VMEM_LIMIT),
        name="in_proj",
    )(x, wqkv, wf, wu, wg, bf)


CS_CHUNK = 256


def _cumsum_kernel(lfm_ref, lf_ref, cm_ref, ck_ref, lft_ref):
    r16 = lax.broadcasted_iota(jnp.int32, (N_META, N_META), 0)
    c16 = lax.broadcasted_iota(jnp.int32, (N_META, N_META), 1)
    tri16 = (c16 <= r16).astype(BF16)
    cm = _dot_exact01(lfm_ref[...], tri16, lhs=False)
    for h in range(HEADS):
        cm_ref[h] = jnp.broadcast_to(cm[:, h:h + 1] * LOG2E, (N_META, LANES))
    carry = cm[N_META - 1:N_META, :]
    rr = lax.broadcasted_iota(jnp.int32, (CS_CHUNK, CS_CHUNK), 0)
    cc = lax.broadcasted_iota(jnp.int32, (CS_CHUNK, CS_CHUNK), 1)
    tri = (cc <= rr).astype(BF16)
    t = lf_ref.shape[1]
    for i in range(t // CS_CHUNK):
        sl = pl.ds(i * CS_CHUNK, CS_CHUNK)
        lf = lf_ref[0, sl, :]
        c = _dot_exact01(lf, tri, lhs=False) + carry
        for h in range(HEADS):
            ck_ref[0, h, sl, :] = jnp.broadcast_to(c[:, h:h + 1] * LOG2E, (CS_CHUNK, LANES))
        lft_ref[0, :, sl] = jnp.transpose(lf)[:HEADS]
        carry = c[CS_CHUNK - 1:CS_CHUNK, :]


def _prompt_cumsum(lf_meta, lf):
    b, t, _ = lf.shape
    return pl.pallas_call(
        _cumsum_kernel,
        grid=(b,),
        in_specs=[_const_spec((N_META, LANES)), pl.BlockSpec((1, t, LANES), lambda i: (i, 0, 0))],
        out_specs=[pl.BlockSpec((HEADS, N_META, LANES), lambda i: (0, 0, 0)),
                   pl.BlockSpec((1, HEADS, t, LANES), lambda i: (i, 0, 0, 0)),
                   pl.BlockSpec((1, HEADS, t), lambda i: (i, 0, 0))],
        out_shape=[jax.ShapeDtypeStruct((HEADS, N_META, LANES), F32),
                   jax.ShapeDtypeStruct((b, HEADS, t, LANES), F32),
                   jax.ShapeDtypeStruct((b, HEADS, t), F32)],
        compiler_params=pltpu.CompilerParams(dimension_semantics=("arbitrary",),
                                             vmem_limit_bytes=VMEM_LIMIT),
        name="prompt_cumsum",
    )(lf_meta, lf)


TQ = 256
TK = 256


def _flash_kernel(q_ref, kb_ref, vta_ref, ck_ref, km_ref, vmta_ref, cm_ref, o_ref, u_scr):
    qi = pl.program_id(2)
    qt = q_ref[0].astype(F32).T
    feat = lax.broadcasted_iota(jnp.int32, (LANES, TQ), 0)
    reps = TQ // LANES
    qts = [jnp.where((feat // HEAD_DIM) == hh, qt, 0.0).astype(BF16) for hh in range(2)]

    def scores(j):
        off = pl.multiple_of(j * TK, TK)
        kblk = kb_ref[0, pl.ds(off, TK), :]
        return [jnp.dot(kblk, qts[hh], preferred_element_type=F32) for hh in range(2)]

    def stage1(j, s):
        off = pl.multiple_of(j * TK, TK)
        mbs = []
        for hh in range(2):
            u = s[hh] - jnp.concatenate([ck_ref[0, hh, pl.ds(off, TK), :]] * reps, axis=1)
            u_scr[j % 2, hh] = u
            mbs.append(jnp.max(u, axis=0, keepdims=True))
        return mbs

    def stage2(u, mb, vta, m):
        m_new = mb if m is None else jnp.maximum(m, mb)
        p = jnp.exp2(u - m_new).astype(BF16)
        pv = jnp.dot(vta, p, preferred_element_type=F32)
        return m_new, pv

    def values(j, hh):
        off = pl.multiple_of(j * TK, TK)
        return vta_ref[0, pl.ds(hh * V_ROWS, V_ROWS), pl.ds(off, TK)]

    mbs = stage1(0, scores(0))

    state = []
    for hh in range(2):
        s = jnp.dot(km_ref[...], qts[hh], preferred_element_type=F32)
        u = s - jnp.concatenate([cm_ref[hh]] * reps, axis=1)
        m, pv = stage2(u, jnp.max(u, axis=0, keepdims=True), vmta_ref[pl.ds(hh * V_ROWS, V_ROWS), :], None)
        state += [m, pv]

    def body(j, carry):
        mbs, state = carry[:2], carry[2:]
        s_next = scores(j + 1)
        new = [stage2(u_scr[j % 2, hh], mbs[hh], values(j, hh), state[2 * hh]) for hh in range(2)]
        out = list(stage1(j + 1, s_next))
        for hh in range(2):
            m, acc = state[2 * hh], state[2 * hh + 1]
            m_new, pv = new[hh]
            out += [m_new, jnp.exp2(m - m_new) * acc + pv]
        return tuple(out)

    carry = lax.fori_loop(0, qi, body, tuple(mbs) + tuple(state))
    state = carry[2:]

    key = lax.broadcasted_iota(jnp.int32, (TK, TQ), 0)
    qry = lax.broadcasted_iota(jnp.int32, (TK, TQ), 1)
    outs = []
    for hh in range(2):
        m, acc = state[2 * hh], state[2 * hh + 1]
        u = jnp.where(key <= qry, u_scr[qi % 2, hh], NEG)
        m_new, pv = stage2(u, jnp.max(u, axis=0, keepdims=True), values(qi, hh), m)
        acc = jnp.exp2(m - m_new) * acc + pv
        outs.append(acc[:HEAD_DIM] / acc[HEAD_DIM:HEAD_DIM + 1])
    o_ref[0] = jnp.concatenate(outs, axis=0).T.astype(o_ref.dtype)


def _prompt_attention(q, kb, vta, ckrep, kmb, vmta, cmrep):
    b, t, _ = q.shape
    assert t % TQ == 0 and TQ == TK
    grid = (b, ATTN_W // LANES, t // TQ)
    return pl.pallas_call(
        _flash_kernel,
        grid=grid,
        in_specs=[
            pl.BlockSpec((1, TQ, LANES), lambda i, p, j: (i, j, p)),
            pl.BlockSpec((1, t, LANES), lambda i, p, j: (i, 0, p)),
            pl.BlockSpec((1, 2 * V_ROWS, t), lambda i, p, j: (i, p, 0)),
            pl.BlockSpec((1, 2, t, LANES), lambda i, p, j: (i, p, 0, 0)),
            pl.BlockSpec((N_META, LANES), lambda i, p, j: (0, p)),
            pl.BlockSpec((2 * V_ROWS, N_META), lambda i, p, j: (p, 0)),
            pl.BlockSpec((2, N_META, LANES), lambda i, p, j: (p, 0, 0)),
        ],
        out_specs=pl.BlockSpec((1, TQ, LANES), lambda i, p, j: (i, j, p)),
        out_shape=jax.ShapeDtypeStruct((b, t, ATTN_W), BF16),
        scratch_shapes=[pltpu.VMEM((2, 2, TK, TQ), F32)],
        compiler_params=pltpu.CompilerParams(dimension_semantics=("arbitrary",) * 3,
                                             vmem_limit_bytes=VMEM_LIMIT),
        name="prompt_attention",
    )(q, kb, vta, ckrep, kmb, vmta, cmrep)


CONV_BT = 256
CONV_SUB = 32
CONV_HALO = 32


def _conv_kernel(prev_ref, cur_ref, meta_ref, w_ref, cb_ref, g_ref, b_ref, o_ref, win_ref):
    i = pl.program_id(1)
    win_ref[pl.ds(0, CONV_HALO), :] = jnp.where(i == 0, meta_ref[...], prev_ref[0])
    win_ref[pl.ds(CONV_HALO, CONV_BT), :] = cur_ref[0]
    w = w_ref[...]
    base = CONV_HALO - TAIL
    for s0 in range(0, CONV_BT, CONV_SUB):
        acc = jnp.zeros((CONV_SUB, CONV_CH), F32)
        for j in range(CONV_W):
            acc = acc + w[j:j + 1, :] * win_ref[pl.ds(s0 + base + j, CONV_SUB), :]
        y = _layer_norm(acc + cb_ref[...], g_ref[...], b_ref[...])
        o_ref[0, pl.ds(s0, CONV_SUB), :] = (y * (1.0 / (1.0 + jnp.exp(-y)))).astype(o_ref.dtype)


def _prompt_conv(glu, meta_halo, conv_w, conv_b, g, b):
    bsz, t, _ = glu.shape
    assert t % CONV_BT == 0
    r = CONV_BT // CONV_HALO
    return pl.pallas_call(
        _conv_kernel,
        grid=(bsz, t // CONV_BT),
        in_specs=[
            pl.BlockSpec((1, CONV_HALO, CONV_CH), lambda i, j: (i, jnp.maximum(j * r - 1, 0), 0)),
            pl.BlockSpec((1, CONV_BT, CONV_CH), lambda i, j: (i, j, 0)),
            _const_spec(meta_halo.shape), _const_spec(conv_w.shape), _const_spec(conv_b.shape),
            _const_spec(g.shape), _const_spec(b.shape),
        ],
        out_specs=pl.BlockSpec((1, CONV_BT, CONV_CH), lambda i, j: (i, j, 0)),
        out_shape=jax.ShapeDtypeStruct((bsz, t, CONV_CH), BF16),
        scratch_shapes=[pltpu.VMEM((CONV_HALO + CONV_BT, CONV_CH), F32)],
        compiler_params=pltpu.CompilerParams(dimension_semantics=("arbitrary",) * 2,
                                             vmem_limit_bytes=VMEM_LIMIT),
        name="prompt_conv",
    )(glu, glu, meta_halo, conv_w, conv_b, g, b)


def _sample_conv_kernel(st_ref, glu_ref, w_ref, cb_ref, g_ref, b_ref, o_ref):
    w = w_ref[...]
    acc = w[TAIL:TAIL + 1, :] * glu_ref[...]
    for j in range(TAIL):
        acc = acc + w[j:j + 1, :] * st_ref[j]
    y = _layer_norm(acc + cb_ref[...], g_ref[...], b_ref[...])
    o_ref[...] = (y * (1.0 / (1.0 + jnp.exp(-y)))).astype(o_ref.dtype)


def _sample_conv(state_t, glus, conv_w, conv_b, g, b):
    n = glus.shape[0]
    return pl.pallas_call(
        _sample_conv_kernel,
        out_shape=jax.ShapeDtypeStruct((n, CONV_CH), BF16),
        compiler_params=pltpu.CompilerParams(vmem_limit_bytes=VMEM_LIMIT),
        name="sample_conv",
    )(state_t, glus, conv_w, conv_b, g, b)


FF_CHUNK = 1024


def _mlp_kernel(x_ref, a_ref, c_ref, woa_ref, woc_ref, g1_ref, b1_ref, wup_ref, wdn_ref, g2_ref, b2_ref,
                o_ref):
    mixed = jnp.dot(a_ref[...], woa_ref[...], preferred_element_type=F32)
    mixed = mixed + jnp.dot(c_ref[...], woc_ref[...], preferred_element_type=F32)
    h = _layer_norm(ALPHA * x_ref[...] + mixed, g1_ref[...], b1_ref[...])
    hb = h.astype(BF16)
    d_ff = wup_ref.shape[1]
    m = None
    for c0 in range(0, d_ff, FF_CHUNK):
        u = jnp.dot(hb, wup_ref[:, pl.ds(c0, FF_CHUNK)], preferred_element_type=F32)
        u = jnp.maximum(u, 0.0)
        u = (u * u).astype(BF16)
        t = jnp.dot(u, wdn_ref[pl.ds(c0, FF_CHUNK), :], preferred_element_type=F32)
        m = t if m is None else m + t
    o_ref[...] = _layer_norm(ALPHA * h + m, g2_ref[...], b2_ref[...])


def _merge_mlp(x, attn, conv, woa, woc, g1, b1, wup, wdn, g2, b2, bm):
    m, d = x.shape
    assert m % bm == 0
    row = lambda w: pl.BlockSpec((bm, w), lambda i: (i, 0))
    consts = [woa, woc, g1, b1, wup, wdn, g2, b2]
    return pl.pallas_call(
        _mlp_kernel,
        grid=(m // bm,),
        in_specs=[row(d), row(ATTN_W), row(CONV_CH)] + [_const_spec(a.shape) for a in consts],
        out_specs=row(d),
        out_shape=jax.ShapeDtypeStruct((m, d), F32),
        compiler_params=pltpu.CompilerParams(dimension_semantics=("arbitrary",),
                                             vmem_limit_bytes=VMEM_LIMIT),
        name="merge_mlp",
    )(x, attn, conv, *consts)


def _paged_bias_kernel(pt_ref, lfn_ref, lft_hbm, d_ref, buf, sem):
    b = pl.program_id(0)
    nb = pl.num_programs(0)
    n_pages = buf.shape[1]

    def copy(seq, slot, p):
        return pltpu.make_async_copy(lft_hbm.at[pt_ref[seq * n_pages + p]], buf.at[slot, p], sem.at[slot])

    def start_all(seq, slot):
        def body(p, _):
            copy(seq, slot, p).start()
            return 0
        lax.fori_loop(0, n_pages, body, 0)

    @pl.when(b == 0)
    def _():
        start_all(0, 0)

    @pl.when(b + 1 < nb)
    def _():
        start_all(b + 1, (b + 1) % 2)

    slot = b % 2

    def wait_body(p, _):
        copy(b, slot, p).wait()
        return 0
    lax.fori_loop(0, n_pages, wait_body, 0)

    rows = n_pages * HEADS
    lf = buf[slot].reshape(rows, PAGE)
    s_in = lax.broadcasted_iota(jnp.int32, (PAGE, PAGE), 0)
    s_out = lax.broadcasted_iota(jnp.int32, (PAGE, PAGE), 1)
    later = (s_in > s_out).astype(BF16)
    ones = jnp.ones((PAGE, PAGE), BF16)
    local = _dot_exact01(lf, later, lhs=True)
    tot = _dot_exact01(lf, ones, lhs=True)
    r_out = lax.broadcasted_iota(jnp.int32, (rows, rows), 0)
    r_in = lax.broadcasted_iota(jnp.int32, (rows, rows), 1)
    later_pages = ((r_in % HEADS == r_out % HEADS) & (r_in // HEADS > r_out // HEADS)).astype(BF16)
    cross = _dot_exact01(tot, later_pages, lhs=False)
    d = (local + cross).reshape(n_pages, HEADS, PAGE) + lfn_ref[0][None]
    d_ref[0] = d.reshape(rows, PAGE) * LOG2E


def _paged_bias(pt_flat, lfn_rep, lft_pool, n_pages):
    n = lfn_rep.shape[0]
    rows = n_pages * HEADS
    grid_spec = pltpu.PrefetchScalarGridSpec(
        num_scalar_prefetch=1,
        grid=(n,),
        in_specs=[pl.BlockSpec((1, HEADS, PAGE), lambda i, pt: (i, 0, 0)),
                  pl.BlockSpec(memory_space=pl.ANY)],
        out_specs=pl.BlockSpec((1, rows, PAGE), lambda i, pt: (i, 0, 0)),
        scratch_shapes=[pltpu.VMEM((2, n_pages, HEADS, PAGE), F32), pltpu.SemaphoreType.DMA((2,))],
    )
    return pl.pallas_call(
        _paged_bias_kernel,
        grid_spec=grid_spec,
        out_shape=jax.ShapeDtypeStruct((n, rows, PAGE), F32),
        compiler_params=pltpu.CompilerParams(dimension_semantics=("arbitrary",),
                                             vmem_limit_bytes=VMEM_LIMIT),
        name="paged_bias",
    )(pt_flat, lfn_rep, lft_pool)


NBUF = 16
PGROUP = 4


def _per_head(col):
    return jnp.broadcast_to(col[:, None, :], (HEADS, HEAD_DIM, 1)).reshape(ATTN_W, 1)


def _paged_attn_kernel(pt_ref, qt_ref, knt_ref, vnt_ref, d_ref, ck_hbm, cv_hbm, o_ref,
                       buf, sem, s_all, qb_ref, acc_ref):
    b = pl.program_id(0)
    nb = pl.num_programs(0)
    n_pages = s_all.shape[0]
    units = 2 * n_pages
    total = nb * units

    def start(unit_global):
        seq = unit_global // units
        u = unit_global % units
        slot = unit_global % NBUF
        page = pt_ref[seq * n_pages + u % n_pages]

        @pl.when(u < n_pages)
        def _():
            pltpu.make_async_copy(ck_hbm.at[page], buf.at[slot], sem.at[slot]).start()

        @pl.when(u >= n_pages)
        def _():
            pltpu.make_async_copy(cv_hbm.at[page], buf.at[slot], sem.at[slot]).start()

    def wait(slot):
        pltpu.make_async_copy(ck_hbm.at[0], buf.at[slot], sem.at[slot]).wait()

    @pl.when(b == 0)
    def _():
        for i in range(NBUF):
            start(i)

    base = b * units

    seq_lane = lax.broadcasted_iota(jnp.int32, qt_ref.shape, 1)
    mine = seq_lane == b
    pick = lambda ref: jnp.sum(jnp.where(mine, ref[...], 0.0), axis=1, keepdims=True)
    qcol, kcol, vcol = pick(qt_ref), pick(knt_ref), pick(vnt_ref)
    qb_ref[...] = jnp.broadcast_to(qcol, (ATTN_W, PAGE))
    s_new = jnp.sum((qcol * kcol).reshape(HEADS, HEAD_DIM, 1), axis=1)

    def key_group(g, _):
        for i in range(PGROUP):
            u = g * PGROUP + i
            slot = (base + u) % NBUF
            wait(slot)
            prod = buf[slot] * qb_ref[...]
            s = jnp.sum(prod.reshape(HEADS, HEAD_DIM, PAGE), axis=1)
            s_all[u] = s + d_ref[0, pl.ds(pl.multiple_of(u * HEADS, HEADS), HEADS), :]

            @pl.when(base + u + NBUF < total)
            def _():
                start(base + u + NBUF)
        return 0

    lax.fori_loop(0, n_pages // PGROUP, key_group, 0)

    s = s_all[...]
    m = jnp.max(jnp.max(s, axis=0), axis=1, keepdims=True)
    m = jnp.maximum(m, s_new)
    p = jnp.exp2(s - m[None])
    l = jnp.sum(jnp.sum(p, axis=0), axis=1, keepdims=True)
    p_new = jnp.exp2(s_new - m)
    l = l + p_new
    s_all[...] = p

    pos = lax.broadcasted_iota(jnp.int32, (ATTN_W, PAGE), 1)
    acc_ref[...] = jnp.where(pos == 0, _per_head(p_new) * vcol, 0.0)

    def value_group(g, _):
        part = None
        for i in range(PGROUP):
            u = g * PGROUP + i
            slot = (base + n_pages + u) % NBUF
            wait(slot)
            pe = jnp.broadcast_to(s_all[u][:, None, :], (HEADS, HEAD_DIM, PAGE)).reshape(ATTN_W, PAGE)
            t = buf[slot] * pe
            part = t if part is None else part + t

            @pl.when(base + n_pages + u + NBUF < total)
            def _():
                start(base + n_pages + u + NBUF)
        acc_ref[...] += part
        return 0

    lax.fori_loop(0, n_pages // PGROUP, value_group, 0)
    ocol = jnp.sum(acc_ref[...], axis=1, keepdims=True) / _per_head(l)

    @pl.when(b == 0)
    def _():
        o_ref[...] = jnp.zeros_like(o_ref)

    o_ref[...] = jnp.where(mine, ocol, o_ref[...])


def _paged_attention(pt_flat, qt, knt, vnt, d_all, cache_kt, cache_vt, n_pages):
    n = qt.shape[1]
    assert n_pages % PGROUP == 0 and (2 * n_pages) % NBUF == 0
    rows = n_pages * HEADS
    full = pl.BlockSpec((ATTN_W, n), lambda i, pt: (0, 0))
    grid_spec = pltpu.PrefetchScalarGridSpec(
        num_scalar_prefetch=1,
        grid=(n,),
        in_specs=[full, full, full,
                  pl.BlockSpec((1, rows, PAGE), lambda i, pt: (i, 0, 0)),
                  pl.BlockSpec(memory_space=pl.ANY), pl.BlockSpec(memory_space=pl.ANY)],
        out_specs=full,
        scratch_shapes=[pltpu.VMEM((NBUF, ATTN_W, PAGE), F32), pltpu.SemaphoreType.DMA((NBUF,)),
                        pltpu.VMEM((n_pages, HEADS, PAGE), F32),
                        pltpu.VMEM((ATTN_W, PAGE), F32), pltpu.VMEM((ATTN_W, PAGE), F32)],
    )
    return pl.pallas_call(
        _paged_attn_kernel,
        grid_spec=grid_spec,
        out_shape=jax.ShapeDtypeStruct((ATTN_W, n), F32),
        compiler_params=pltpu.CompilerParams(dimension_semantics=("arbitrary",),
                                             vmem_limit_bytes=VMEM_LIMIT),
        name="paged_attention",
    )(pt_flat, qt, knt, vnt, d_all, cache_kt, cache_vt)


def _pick_bm(m, target):
    bm = min(m, target)
    while m % bm:
        bm //= 2
    return bm


def _positions_minor(x):
    nd = x.ndim
    return jnp.transpose(x, (0, 1, nd - 1) + tuple(range(2, nd - 1)))


def kernel(x_prompt, x_sample, cache_k, cache_v, cache_logf, state_conv, page_table, meta_tokens, w_in, b_f,
           conv_w, conv_b, conv_ln_g, conv_ln_b, w_out, ln1_g, ln1_b, w_up, w_down, ln2_g, ln2_b):
    bsz, seq, d = x_prompt.shape
    dbsz, dseq, _ = x_sample.shape
    depth = w_in.shape[0]
    assert depth == 1 and dseq == 1
    n_pool = cache_k.shape[1]
    n_pages = page_table.shape[1]

    w = w_in[0]
    o_f = 3 * ATTN_W
    wqkv = w[:, :o_f].astype(BF16)
    wf = jnp.pad(w[:, o_f:o_f + HEADS], ((0, 0), (0, LANES - HEADS))).astype(BF16)
    wu = w[:, o_f + HEADS:o_f + HEADS + CONV_CH].astype(BF16)
    wg = w[:, o_f + HEADS + CONV_CH:].astype(BF16)
    bf = jnp.pad(b_f[0], (0, LANES - HEADS)).reshape(1, LANES)
    woa = w_out[0, :ATTN_W].astype(BF16)
    woc = w_out[0, ATTN_W:].astype(BF16)
    wup = w_up[0].astype(BF16)
    wdn = w_down[0].astype(BF16)
    row = lambda a: a[0].reshape(1, -1)
    cw, cb, cg, cbeta = conv_w[0], row(conv_b), row(conv_ln_g), row(conv_ln_b)
    g1, b1, g2, b2 = row(ln1_g), row(ln1_b), row(ln2_g), row(ln2_b)
    proj = functools.partial(_in_proj, wqkv=wqkv, wf=wf, wu=wu, wg=wg, bf=bf)

    q, kb, kt, vt, vta, lf, glu = proj(x_prompt, bm=_pick_bm(seq, 512), sample=False)
    meta_pad = jnp.pad(meta_tokens, ((0, LANES - N_META), (0, 0)))[None]
    _, kmb, kmt, vmt, vmta, lfm, glum = proj(meta_pad, bm=LANES, sample=False)
    lfm = lfm[0, :N_META]
    cmrep, ckrep, lft = _prompt_cumsum(lfm, lf)
    attn = _prompt_attention(q, kb, vta, ckrep, kmb[0, :N_META], vmta[0, :, :N_META], cmrep)
    meta_halo = jnp.concatenate([jnp.zeros((CONV_HALO - N_META, CONV_CH), F32), glum[0, :N_META]], axis=0)
    conv = _prompt_conv(glu, meta_halo, cw, cb, cg, cbeta)
    y_prompt = _merge_mlp(x_prompt.reshape(bsz * seq, d), attn.reshape(bsz * seq, ATTN_W),
                          conv.reshape(bsz * seq, CONV_CH), woa, woc, g1, b1, wup, wdn, g2, b2,
                          _pick_bm(bsz * seq, 512))
    y_prompt = y_prompt.reshape(bsz, seq, d)

    def with_meta(meta_t, body_t):
        rows = body_t.shape[1]
        meta_b = jnp.broadcast_to(meta_t[None, :rows, :N_META], (bsz, rows, N_META))
        return jnp.concatenate([meta_b, body_t], axis=2)

    k_prompt = _positions_minor(with_meta(kmt[0], kt).reshape(1, bsz, HEADS, HEAD_DIM, N_META + seq))
    v_prompt = _positions_minor(with_meta(vmt[0], vt).reshape(1, bsz, HEADS, HEAD_DIM, N_META + seq))
    logf_prompt = _positions_minor(with_meta(lfm.T, lft).reshape(1, bsz, HEADS, N_META + seq))
    conv_prompt = glu[:, seq - TAIL:, :].reshape(1, bsz, TAIL, CONV_CH)

    qst, kst, vst, lfs, glus = proj(x_sample.reshape(1, dbsz, d), bm=dbsz, sample=True)
    lfs, glus = lfs[0], glus[0]
    pt_flat = page_table.reshape(-1)
    cache_kt = jnp.transpose(cache_k, (0, 1, 3, 4, 2)).reshape(n_pool, ATTN_W, PAGE)
    cache_vt = jnp.transpose(cache_v, (0, 1, 3, 4, 2)).reshape(n_pool, ATTN_W, PAGE)
    lft_pool = jnp.transpose(cache_logf, (0, 1, 3, 2)).reshape(n_pool, HEADS, PAGE)
    lfn_rep = jnp.broadcast_to(lfs[:, :HEADS, None], (dbsz, HEADS, PAGE))
    d_all = _paged_bias(pt_flat, lfn_rep, lft_pool, n_pages)
    attn_st = _paged_attention(pt_flat, qst[0], kst[0], vst[0], d_all, cache_kt, cache_vt, n_pages)
    state_t = jnp.transpose(state_conv[0], (1, 0, 2))
    conv_s = _sample_conv(state_t, glus, cw, cb, cg, cbeta)
    y_sample = _merge_mlp(x_sample.reshape(dbsz, d), attn_st.T.astype(BF16), conv_s,
                          woa, woc, g1, b1, wup, wdn, g2, b2, dbsz)
    y_sample = y_sample.reshape(dbsz, 1, d)

    k_sample = jnp.transpose(kst.reshape(1, 1, HEADS, HEAD_DIM, dbsz), (0, 4, 1, 2, 3))
    v_sample = jnp.transpose(vst.reshape(1, 1, HEADS, HEAD_DIM, dbsz), (0, 4, 1, 2, 3))
    logf_sample = lfs[:, :HEADS].reshape(1, dbsz, 1, HEADS)
    conv_sample = jnp.transpose(jnp.concatenate([state_t[1:], glus[None]], axis=0), (1, 0, 2))[None]

    return (y_prompt, y_sample, k_prompt, v_prompt, logf_prompt, conv_prompt,
            k_sample, v_sample, logf_sample, conv_sample)
```

```python
import functools
import math

import jax
import jax.numpy as jnp
from jax import lax
from jax.experimental import pallas as pl
from jax.experimental.pallas import tpu as pltpu

N_META = 16
HEADS = 8
HEAD_DIM = 64
ATTN_W = HEADS * HEAD_DIM
CONV_CH = 512
CONV_W = 31
TAIL = CONV_W - 1
PAGE = 128
LANES = 128
SUBLANES = 8
BF16_ROWS = 16
ALPHA = 2.0 ** 0.25
LN_EPS = 1e-5
NEG = -1e30
LOG2E = math.log2(math.e)
QSCALE = HEAD_DIM ** -0.5 * LOG2E

F32 = jnp.float32
BF16 = jnp.bfloat16

VMEM_LIMIT = 56 * 1024 * 1024


def _const_spec(shape):
    nd = len(shape)
    return pl.BlockSpec(shape, lambda *_: (0,) * nd, pipeline_mode=pl.Buffered(1))


def _split3(x):
    hi = x.astype(BF16)
    r1 = x - hi.astype(F32)
    mid = r1.astype(BF16)
    lo = (r1 - mid.astype(F32)).astype(BF16)
    return hi, mid, lo


def _dot_exact01(x, m01, *, lhs=True):
    out = None
    for piece in _split3(x):
        if lhs:
            t = jnp.dot(piece, m01, preferred_element_type=F32)
        else:
            t = jnp.dot(m01, piece, preferred_element_type=F32)
        out = t if out is None else out + t
    return out


def _layer_norm(x, g, b):
    mu = jnp.mean(x, axis=-1, keepdims=True)
    xc = x - mu
    var = jnp.mean(xc * xc, axis=-1, keepdims=True)
    return xc * lax.rsqrt(var + LN_EPS) * g + b


def _silu(y):
    return y * (1.0 / (1.0 + jnp.exp(-y)))


V_ROWS = HEAD_DIM + BF16_ROWS
CONV_SUB = 32
CONV_CHAINS = 1
CONV_HALO = 32


def _in_proj_kernel(*refs, sample, conv, tiles_per_seq):
    x_ref, wqkv_ref, wf_ref, wu_ref, wg_ref, bf_ref = refs[:6]
    refs = refs[6:]
    if conv:
        halo_ref, cw_ref, cb_ref, cg_ref, cbeta_ref = refs[:5]
        refs = refs[5:]
        conv_ref, win_ref, sh_ref = refs[-3:]
        refs = refs[:-3]
    bm = x_ref.shape[0]

    conv_steps = []
    if conv:
        g = pl.program_id(0)

        @pl.when(g == 0)
        def _():
            win_ref[...] = jnp.zeros_like(win_ref)

        first_of_seq = ((g - 1) % tiles_per_seq) == 0
        win_ref[pl.ds(0, CONV_HALO), :] = jnp.where(first_of_seq, halo_ref[...], win_ref[pl.ds(0, CONV_HALO), :])
        span = CONV_HALO + bm - SUBLANES
        for r in range(1, SUBLANES):
            sh_ref[r, pl.ds(0, span), :] = win_ref[pl.ds(r, span), :]
        base = CONV_HALO - TAIL

        def conv_piece(s0):
            groups = CONV_SUB // SUBLANES
            accs = [None] * CONV_CHAINS
            for j in range(CONV_W):
                off = s0 + base + j
                r, a = off % SUBLANES, off - off % SUBLANES
                src = win_ref[pl.ds(a, CONV_SUB), :] if r == 0 else sh_ref[r, pl.ds(a, CONV_SUB), :]
                t = cw_ref[j][None] * src.reshape(groups, SUBLANES, CONV_CH)
                c = j % CONV_CHAINS
                accs[c] = t if accs[c] is None else accs[c] + t
            while len(accs) > 1:
                accs = [accs[i] + accs[i + 1] for i in range(0, len(accs), 2)]
            y = _layer_norm(accs[0].reshape(CONV_SUB, CONV_CH) + cb_ref[...], cg_ref[...], cbeta_ref[...])
            conv_ref[pl.ds(s0, CONV_SUB), :] = _silu(y).astype(conv_ref.dtype)

        conv_steps = list(range(0, bm, CONV_SUB))

    def run_conv(n):
        for _ in range(min(n, len(conv_steps))):
            conv_piece(conv_steps.pop(0))

    per_dot = -(-len(conv_steps) // 5) if conv else 0
    x = x_ref[...].astype(BF16)
    dot = lambda w: jnp.dot(x, w, preferred_element_type=F32)

    q = dot(wqkv_ref[:, pl.ds(0, ATTN_W)]) * QSCALE
    if sample:
        qt_ref, kt_ref, vt_ref, lf_ref, glu_ref = refs
        qt_ref[0] = q.T
    else:
        qt_ref, kb_ref, kt_ref, vt_ref, vta_ref, lf_ref, glu_ref = refs
        qt_ref[0] = q.T.astype(BF16)
    run_conv(per_dot)

    kv = dot(wqkv_ref[:, pl.ds(ATTN_W, 2 * ATTN_W)])
    k = kv[:, :ATTN_W]
    kt_ref[0] = k.T
    if not sample:
        kb_ref[...] = k.astype(BF16)
    run_conv(2 * per_dot)

    vt = kv[:, ATTN_W:].T
    vt_ref[0] = vt
    if not sample:
        ones = jnp.ones((BF16_ROWS, bm), BF16)
        for h in range(HEADS):
            vta_ref[0, pl.ds(h * V_ROWS, HEAD_DIM), :] = vt[h * HEAD_DIM:(h + 1) * HEAD_DIM].astype(BF16)
            vta_ref[0, pl.ds(h * V_ROWS + HEAD_DIM, BF16_ROWS), :] = ones
    run_conv(per_dot)

    f = dot(wf_ref[...]) + bf_ref[...]
    lf_ref[...] = jnp.minimum(f, 0.0) - jnp.log(1.0 + jnp.exp(-jnp.abs(f)))
    u = dot(wu_ref[...])
    run_conv(per_dot)
    gate = dot(wg_ref[...])
    run_conv(len(conv_steps))
    glu = u * (1.0 / (1.0 + jnp.exp(-gate)))
    glu_ref[...] = glu
    if conv:
        hist = win_ref[pl.ds(bm, CONV_HALO), :]
        win_ref[pl.ds(0, CONV_HALO), :] = hist
        win_ref[pl.ds(CONV_HALO, bm), :] = glu


def _in_proj(x, wqkv, wf, wu, wg, bf, bm, sample=False, conv_args=None):
    b, t, d = x.shape
    assert t % bm == 0
    nt = t // bm
    tiles = b * nt
    conv = conv_args is not None
    cur = lambda g: jnp.minimum(g, tiles - 1)
    row = lambda w: pl.BlockSpec((bm, w), lambda g: (cur(g), 0))
    col = lambda r: pl.BlockSpec((1, r, bm), lambda g: (cur(g) // nt, 0, cur(g) % nt))
    nat = lambda w, dt: jax.ShapeDtypeStruct((b * t, w), dt)
    tr = lambda r, dt: jax.ShapeDtypeStruct((b, r, t), dt)
    if sample:
        outs = [tr(ATTN_W, F32), tr(ATTN_W, F32), tr(ATTN_W, F32), nat(LANES, F32), nat(CONV_CH, F32)]
        specs = [col(ATTN_W), col(ATTN_W), col(ATTN_W), row(LANES), row(CONV_CH)]
    else:
        outs = [tr(ATTN_W, BF16), nat(ATTN_W, BF16), tr(ATTN_W, F32), tr(ATTN_W, F32),
                tr(HEADS * V_ROWS, BF16), nat(LANES, F32), nat(CONV_CH, F32)]
        specs = [col(ATTN_W), row(ATTN_W), col(ATTN_W), col(ATTN_W), col(HEADS * V_ROWS), row(LANES),
                 row(CONV_CH)]
    args = [x.reshape(b * t, d), wqkv, wf, wu, wg, bf]
    in_specs = [row(d)] + [_const_spec(a.shape) for a in args[1:]]
    scratch = []
    if conv:
        args += list(conv_args)
        in_specs += [_const_spec(a.shape) for a in conv_args]
        outs.append(nat(CONV_CH, BF16))
        specs.append(pl.BlockSpec((bm, CONV_CH), lambda g: (jnp.maximum(g - 1, 0), 0)))
        scratch = [pltpu.VMEM((CONV_HALO + bm, CONV_CH), F32),
                   pltpu.VMEM((SUBLANES, CONV_HALO + bm, CONV_CH), F32)]
    return pl.pallas_call(
        functools.partial(_in_proj_kernel, sample=sample, conv=conv, tiles_per_seq=nt),
        grid=(tiles + 1 if conv else tiles,),
        in_specs=in_specs,
        out_specs=specs,
        out_shape=outs,
        scratch_shapes=scratch,
        compiler_params=pltpu.CompilerParams(dimension_semantics=("arbitrary",),
                                             vmem_limit_bytes=VMEM_LIMIT),
        name="in_proj",
    )(*args)


CS_CHUNK = 256


def _cumsum_kernel(lfm_ref, lf_ref, cm_ref, ck_ref, lft_ref):
    r16 = lax.broadcasted_iota(jnp.int32, (N_META, N_META), 0)
    c16 = lax.broadcasted_iota(jnp.int32, (N_META, N_META), 1)
    tri16 = (c16 <= r16).astype(BF16)
    cm = _dot_exact01(lfm_ref[...], tri16, lhs=False)
    for h in range(HEADS):
        cm_ref[h] = jnp.broadcast_to(cm[:, h:h + 1] * LOG2E, (N_META, LANES))
    carry = cm[N_META - 1:N_META, :]
    rr = lax.broadcasted_iota(jnp.int32, (CS_CHUNK, CS_CHUNK), 0)
    cc = lax.broadcasted_iota(jnp.int32, (CS_CHUNK, CS_CHUNK), 1)
    tri = (cc <= rr).astype(BF16)
    t = lf_ref.shape[1]
    for i in range(t // CS_CHUNK):
        sl = pl.ds(i * CS_CHUNK, CS_CHUNK)
        lf = lf_ref[0, sl, :]
        c = _dot_exact01(lf, tri, lhs=False) + carry
        for h in range(HEADS):
            ck_ref[0, h, sl, :] = jnp.broadcast_to(c[:, h:h + 1] * LOG2E, (CS_CHUNK, LANES))
        lft_ref[0, :, sl] = jnp.transpose(lf)[:HEADS]
        carry = c[CS_CHUNK - 1:CS_CHUNK, :]


def _prompt_cumsum(lf_meta, lf):
    b, t, _ = lf.shape
    return pl.pallas_call(
        _cumsum_kernel,
        grid=(b,),
        in_specs=[_const_spec((N_META, LANES)), pl.BlockSpec((1, t, LANES), lambda i: (i, 0, 0))],
        out_specs=[pl.BlockSpec((HEADS, N_META, LANES), lambda i: (0, 0, 0)),
                   pl.BlockSpec((1, HEADS, t, LANES), lambda i: (i, 0, 0, 0)),
                   pl.BlockSpec((1, HEADS, t), lambda i: (i, 0, 0))],
        out_shape=[jax.ShapeDtypeStruct((HEADS, N_META, LANES), F32),
                   jax.ShapeDtypeStruct((b, HEADS, t, LANES), F32),
                   jax.ShapeDtypeStruct((b, HEADS, t), F32)],
        compiler_params=pltpu.CompilerParams(dimension_semantics=("arbitrary",),
                                             vmem_limit_bytes=VMEM_LIMIT),
        name="prompt_cumsum",
    )(lf_meta, lf)


TQ = 256
TK = 256


def _flash_kernel(qt_ref, kb_ref, vta_ref, ck_ref, km_ref, vmta_ref, cm_ref, o_ref, u_scr):
    qi = pl.program_id(2)
    qt = qt_ref[0].astype(F32)
    feat = lax.broadcasted_iota(jnp.int32, (LANES, TQ), 0)
    reps = TQ // LANES
    qts = [jnp.where((feat // HEAD_DIM) == hh, qt, 0.0).astype(BF16) for hh in range(2)]

    def scores(j):
        off = pl.multiple_of(j * TK, TK)
        kblk = kb_ref[0, pl.ds(off, TK), :]
        return [jnp.dot(kblk, qts[hh], preferred_element_type=F32) for hh in range(2)]

    def stage1(j, s):
        off = pl.multiple_of(j * TK, TK)
        mbs = []
        for hh in range(2):
            u = s[hh] - jnp.concatenate([ck_ref[0, hh, pl.ds(off, TK), :]] * reps, axis=1)
            u_scr[j % 2, hh] = u
            mbs.append(jnp.max(u, axis=0, keepdims=True))
        return mbs

    def stage2(u, m_new, vta):
        p = jnp.exp2(u - m_new).astype(BF16)
        return jnp.dot(vta, p, preferred_element_type=F32)

    def values(j, hh):
        off = pl.multiple_of(j * TK, TK)
        return vta_ref[0, pl.ds(hh * V_ROWS, V_ROWS), pl.ds(off, TK)]

    s0 = scores(0)
    s_meta = [jnp.dot(km_ref[...], qts[hh], preferred_element_type=F32) for hh in range(2)]
    mbs = stage1(0, s0)
    u_meta = [s_meta[hh] - jnp.concatenate([cm_ref[hh]] * reps, axis=1) for hh in range(2)]
    m_meta = [jnp.max(u, axis=0, keepdims=True) for u in u_meta]

    def body(j, carry):
        mbs, state = carry[:2], carry[2:]
        s_next = scores(j + 1)
        out = []
        for hh in range(2):
            m, acc = state[2 * hh], state[2 * hh + 1]
            m_new = jnp.maximum(m, mbs[hh])
            pv = stage2(u_scr[j % 2, hh], m_new, values(j, hh))
            out += [m_new, jnp.exp2(m - m_new) * acc + pv]
        return tuple(stage1(j + 1, s_next)) + tuple(out)

    empty = (jnp.full((1, TQ), NEG, F32), jnp.zeros((V_ROWS, TQ), F32))
    carry = lax.fori_loop(0, qi, body, tuple(mbs) + empty + empty)
    state = carry[2:]

    pv_meta = [stage2(u_meta[hh], m_meta[hh], vmta_ref[pl.ds(hh * V_ROWS, V_ROWS), :]) for hh in range(2)]
    key = lax.broadcasted_iota(jnp.int32, (TK, TQ), 0)
    qry = lax.broadcasted_iota(jnp.int32, (TK, TQ), 1)
    outs = []
    for hh in range(2):
        m, acc = state[2 * hh], state[2 * hh + 1]
        u = jnp.where(key <= qry, u_scr[qi % 2, hh], NEG)
        m_new = jnp.maximum(jnp.maximum(m, jnp.max(u, axis=0, keepdims=True)), m_meta[hh])
        pv = stage2(u, m_new, values(qi, hh))
        acc = jnp.exp2(m - m_new) * acc + pv + jnp.exp2(m_meta[hh] - m_new) * pv_meta[hh]
        outs.append(acc[:HEAD_DIM] / acc[HEAD_DIM:HEAD_DIM + 1])
    o_ref[0] = jnp.concatenate(outs, axis=0).T.astype(o_ref.dtype)


def _prompt_attention(qt, kb, vta, ckrep, kmb, vmta, cmrep):
    b, _, t = qt.shape
    assert t % TQ == 0 and TQ == TK
    grid = (b, ATTN_W // LANES, t // TQ)
    return pl.pallas_call(
        _flash_kernel,
        grid=grid,
        in_specs=[
            pl.BlockSpec((1, LANES, TQ), lambda i, p, j: (i, p, j)),
            pl.BlockSpec((1, t, LANES), lambda i, p, j: (i, 0, p)),
            pl.BlockSpec((1, 2 * V_ROWS, t), lambda i, p, j: (i, p, 0)),
            pl.BlockSpec((1, 2, t, LANES), lambda i, p, j: (i, p, 0, 0)),
            pl.BlockSpec((N_META, LANES), lambda i, p, j: (0, p)),
            pl.BlockSpec((2 * V_ROWS, N_META), lambda i, p, j: (p, 0)),
            pl.BlockSpec((2, N_META, LANES), lambda i, p, j: (p, 0, 0)),
        ],
        out_specs=pl.BlockSpec((1, TQ, LANES), lambda i, p, j: (i, j, p)),
        out_shape=jax.ShapeDtypeStruct((b, t, ATTN_W), BF16),
        scratch_shapes=[pltpu.VMEM((2, 2, TK, TQ), F32)],
        compiler_params=pltpu.CompilerParams(dimension_semantics=("arbitrary",) * 3,
                                             vmem_limit_bytes=VMEM_LIMIT),
        name="prompt_attention",
    )(qt, kb, vta, ckrep, kmb, vmta, cmrep)


def _sample_conv_kernel(st_ref, glu_ref, w_ref, cb_ref, g_ref, b_ref, o_ref):
    w = w_ref[...]
    acc = w[TAIL:TAIL + 1, :] * glu_ref[...]
    for j in range(TAIL):
        acc = acc + w[j:j + 1, :] * st_ref[j]
    y = _layer_norm(acc + cb_ref[...], g_ref[...], b_ref[...])
    o_ref[...] = _silu(y).astype(o_ref.dtype)


def _sample_conv(state_t, glus, conv_w, conv_b, g, b):
    n = glus.shape[0]
    return pl.pallas_call(
        _sample_conv_kernel,
        out_shape=jax.ShapeDtypeStruct((n, CONV_CH), BF16),
        compiler_params=pltpu.CompilerParams(vmem_limit_bytes=VMEM_LIMIT),
        name="sample_conv",
    )(state_t, glus, conv_w, conv_b, g, b)


FF_CHUNK = 1024


def _mlp_kernel(x_ref, a_ref, c_ref, woa_ref, woc_ref, g1_ref, b1_ref, wup_ref, wdn_ref, g2_ref, b2_ref,
                o_ref):
    mixed = jnp.dot(a_ref[...], woa_ref[...], preferred_element_type=F32)
    mixed = mixed + jnp.dot(c_ref[...], woc_ref[...], preferred_element_type=F32)
    h = _layer_norm(ALPHA * x_ref[...] + mixed, g1_ref[...], b1_ref[...])
    hb = h.astype(BF16)
    d_ff = wup_ref.shape[1]
    m = None
    for c0 in range(0, d_ff, FF_CHUNK):
        u = jnp.dot(hb, wup_ref[:, pl.ds(c0, FF_CHUNK)], preferred_element_type=F32)
        u = jnp.maximum(u, 0.0)
        u = (u * u).astype(BF16)
        t = jnp.dot(u, wdn_ref[pl.ds(c0, FF_CHUNK), :], preferred_element_type=F32)
        m = t if m is None else m + t
    o_ref[...] = _layer_norm(ALPHA * h + m, g2_ref[...], b2_ref[...])


def _merge_mlp(x, attn, conv, woa, woc, g1, b1, wup, wdn, g2, b2, bm):
    m, d = x.shape
    assert m % bm == 0
    row = lambda w: pl.BlockSpec((bm, w), lambda i: (i, 0))
    consts = [woa, woc, g1, b1, wup, wdn, g2, b2]
    return pl.pallas_call(
        _mlp_kernel,
        grid=(m // bm,),
        in_specs=[row(d), row(ATTN_W), row(CONV_CH)] + [_const_spec(a.shape) for a in consts],
        out_specs=row(d),
        out_shape=jax.ShapeDtypeStruct((m, d), F32),
        compiler_params=pltpu.CompilerParams(dimension_semantics=("arbitrary",),
                                             vmem_limit_bytes=VMEM_LIMIT),
        name="merge_mlp",
    )(x, attn, conv, *consts)


NBUF = 32
PGROUP = 4


def _per_head(col):
    return jnp.broadcast_to(col[:, None, :], (HEADS, HEAD_DIM, 1)).reshape(ATTN_W, 1)


def _paged_attn_kernel(pt_ref, qt_ref, knt_ref, vnt_ref, lfn_ref, lft_hbm, ck_hbm, cv_hbm, o_ref,
                       buf, sem, lfbuf, lfsem, d_scr, s_all, qb_ref, acc_ref):
    b = pl.program_id(0)
    nb = pl.num_programs(0)
    n_pages = s_all.shape[0]
    rows = n_pages * HEADS
    units = 2 * n_pages
    total = nb * units

    def start(unit_global):
        seq = unit_global // units
        u = unit_global % units
        slot = unit_global % NBUF
        page = pt_ref[seq * n_pages + u % n_pages]

        @pl.when(u < n_pages)
        def _():
            pltpu.make_async_copy(ck_hbm.at[page], buf.at[slot], sem.at[slot]).start()

        @pl.when(u >= n_pages)
        def _():
            pltpu.make_async_copy(cv_hbm.at[page], buf.at[slot], sem.at[slot]).start()

    def wait(slot):
        pltpu.make_async_copy(ck_hbm.at[0], buf.at[slot], sem.at[slot]).wait()

    def lf_copy(seq, slot, p):
        return pltpu.make_async_copy(lft_hbm.at[pt_ref[seq * n_pages + p]], lfbuf.at[slot, p], lfsem.at[slot])

    def lf_start(seq, slot):
        def body(p, _):
            lf_copy(seq, slot, p).start()
            return 0
        lax.fori_loop(0, n_pages, body, 0)

    @pl.when(b == 0)
    def _():
        for i in range(NBUF):
            start(i)
        lf_start(0, 0)

    @pl.when(b + 1 < nb)
    def _():
        lf_start(b + 1, (b + 1) % 2)

    def lf_wait(p, _):
        lf_copy(b, b % 2, p).wait()
        return 0
    lax.fori_loop(0, n_pages, lf_wait, 0)

    base = b * units

    lf = lfbuf[b % 2].reshape(rows, PAGE)
    s_in = lax.broadcasted_iota(jnp.int32, (PAGE, PAGE), 0)
    s_out = lax.broadcasted_iota(jnp.int32, (PAGE, PAGE), 1)
    later = (s_in > s_out).astype(BF16)
    local = _dot_exact01(lf, later, lhs=True)
    tot = _dot_exact01(lf, jnp.ones((PAGE, PAGE), BF16), lhs=True)
    r_out = lax.broadcasted_iota(jnp.int32, (rows, rows), 0)
    r_in = lax.broadcasted_iota(jnp.int32, (rows, rows), 1)
    later_pages = ((r_in % HEADS == r_out % HEADS) & (r_in // HEADS > r_out // HEADS)).astype(BF16)
    cross = _dot_exact01(tot, later_pages, lhs=False)
    d = (local + cross).reshape(n_pages, HEADS, PAGE) + lfn_ref[0][None]
    d_scr[...] = d.reshape(rows, PAGE) * LOG2E

    seq_lane = lax.broadcasted_iota(jnp.int32, qt_ref.shape, 1)
    mine = seq_lane == b
    pick = lambda ref: jnp.sum(jnp.where(mine, ref[...], 0.0), axis=1, keepdims=True)
    qcol, kcol, vcol = pick(qt_ref), pick(knt_ref), pick(vnt_ref)
    qb_ref[...] = jnp.broadcast_to(qcol, (ATTN_W, PAGE))
    s_new = jnp.sum((qcol * kcol).reshape(HEADS, HEAD_DIM, 1), axis=1)

    def key_group(g, _):
        for i in range(PGROUP):
            u = g * PGROUP + i
            slot = (base + u) % NBUF
            wait(slot)
            prod = buf[slot] * qb_ref[...]
            s = jnp.sum(prod.reshape(HEADS, HEAD_DIM, PAGE), axis=1)
            s_all[u] = s + d_scr[pl.ds(pl.multiple_of(u * HEADS, HEADS), HEADS), :]

            @pl.when(base + u + NBUF < total)
            def _():
                start(base + u + NBUF)
        return 0

    lax.fori_loop(0, n_pages // PGROUP, key_group, 0)

    s = s_all[...]
    m = jnp.max(jnp.max(s, axis=0), axis=1, keepdims=True)
    m = jnp.maximum(m, s_new)
    p = jnp.exp2(s - m[None])
    l = jnp.sum(jnp.sum(p, axis=0), axis=1, keepdims=True)
    p_new = jnp.exp2(s_new - m)
    l = l + p_new
    s_all[...] = p

    pos = lax.broadcasted_iota(jnp.int32, (ATTN_W, PAGE), 1)
    acc_ref[...] = jnp.where(pos == 0, _per_head(p_new) * vcol, 0.0)

    def value_group(g, _):
        part = None
        for i in range(PGROUP):
            u = g * PGROUP + i
            slot = (base + n_pages + u) % NBUF
            wait(slot)
            pe = jnp.broadcast_to(s_all[u][:, None, :], (HEADS, HEAD_DIM, PAGE)).reshape(ATTN_W, PAGE)
            t = buf[slot] * pe
            part = t if part is None else part + t

            @pl.when(base + n_pages + u + NBUF < total)
            def _():
                start(base + n_pages + u + NBUF)
        acc_ref[...] += part
        return 0

    lax.fori_loop(0, n_pages // PGROUP, value_group, 0)
    ocol = jnp.sum(acc_ref[...], axis=1, keepdims=True) / _per_head(l)

    @pl.when(b == 0)
    def _():
        o_ref[...] = jnp.zeros_like(o_ref)

    o_ref[...] = jnp.where(mine, ocol, o_ref[...])


def _paged_attention(pt_flat, qt, knt, vnt, lfn_rep, lft_pool, cache_kt, cache_vt, n_pages):
    n = qt.shape[1]
    assert n_pages % PGROUP == 0 and 2 * n_pages * n >= NBUF
    rows = n_pages * HEADS
    full = pl.BlockSpec((ATTN_W, n), lambda i, pt: (0, 0))
    hbm = pl.BlockSpec(memory_space=pl.ANY)
    grid_spec = pltpu.PrefetchScalarGridSpec(
        num_scalar_prefetch=1,
        grid=(n,),
        in_specs=[full, full, full, pl.BlockSpec((1, HEADS, PAGE), lambda i, pt: (i, 0, 0)), hbm, hbm, hbm],
        out_specs=full,
        scratch_shapes=[pltpu.VMEM((NBUF, ATTN_W, PAGE), F32), pltpu.SemaphoreType.DMA((NBUF,)),
                        pltpu.VMEM((2, n_pages, HEADS, PAGE), F32), pltpu.SemaphoreType.DMA((2,)),
                        pltpu.VMEM((rows, PAGE), F32),
                        pltpu.VMEM((n_pages, HEADS, PAGE), F32),
                        pltpu.VMEM((ATTN_W, PAGE), F32), pltpu.VMEM((ATTN_W, PAGE), F32)],
    )
    return pl.pallas_call(
        _paged_attn_kernel,
        grid_spec=grid_spec,
        out_shape=jax.ShapeDtypeStruct((ATTN_W, n), F32),
        compiler_params=pltpu.CompilerParams(dimension_semantics=("arbitrary",),
                                             vmem_limit_bytes=VMEM_LIMIT),
        name="paged_attention",
    )(pt_flat, qt, knt, vnt, lfn_rep, lft_pool, cache_kt, cache_vt)


def _pick_bm(m, target):
    bm = min(m, target)
    while m % bm:
        bm //= 2
    return bm


def _positions_minor(x):
    nd = x.ndim
    return jnp.transpose(x, (0, 1, nd - 1) + tuple(range(2, nd - 1)))


def kernel(x_prompt, x_sample, cache_k, cache_v, cache_logf, state_conv, page_table, meta_tokens, w_in, b_f,
           conv_w, conv_b, conv_ln_g, conv_ln_b, w_out, ln1_g, ln1_b, w_up, w_down, ln2_g, ln2_b):
    bsz, seq, d = x_prompt.shape
    dbsz, dseq, _ = x_sample.shape
    depth = w_in.shape[0]
    assert depth == 1 and dseq == 1
    n_pool = cache_k.shape[1]
    n_pages = page_table.shape[1]

    w = w_in[0]
    o_f = 3 * ATTN_W
    wqkv = w[:, :o_f].astype(BF16)
    wf = jnp.pad(w[:, o_f:o_f + HEADS], ((0, 0), (0, LANES - HEADS))).astype(BF16)
    wu = w[:, o_f + HEADS:o_f + HEADS + CONV_CH].astype(BF16)
    wg = w[:, o_f + HEADS + CONV_CH:].astype(BF16)
    bf = jnp.pad(b_f[0], (0, LANES - HEADS)).reshape(1, LANES)
    woa = w_out[0, :ATTN_W].astype(BF16)
    woc = w_out[0, ATTN_W:].astype(BF16)
    wup = w_up[0].astype(BF16)
    wdn = w_down[0].astype(BF16)
    row = lambda a: a[0].reshape(1, -1)
    cw, cb, cg, cbeta = conv_w[0], row(conv_b), row(conv_ln_g), row(conv_ln_b)
    cw_rep = jnp.broadcast_to(cw[:, None, :], (CONV_W, SUBLANES, CONV_CH))
    g1, b1, g2, b2 = row(ln1_g), row(ln1_b), row(ln2_g), row(ln2_b)
    proj = functools.partial(_in_proj, wqkv=wqkv, wf=wf, wu=wu, wg=wg, bf=bf)

    meta_pad = jnp.pad(meta_tokens, ((0, LANES - N_META), (0, 0)))[None]
    _, kmb, kmt, vmt, vmta, lfm, glum = proj(meta_pad, bm=LANES)
    lfm = lfm[:N_META]
    meta_halo = jnp.concatenate([jnp.zeros((CONV_HALO - N_META, CONV_CH), F32), glum[:N_META]], axis=0)
    qt, kb, kt, vt, vta, lf, glu, conv = proj(x_prompt, bm=_pick_bm(seq, 512),
                                              conv_args=(meta_halo, cw_rep, cb, cg, cbeta))
    cmrep, ckrep, lft = _prompt_cumsum(lfm, lf.reshape(bsz, seq, LANES))
    attn = _prompt_attention(qt, kb.reshape(bsz, seq, ATTN_W), vta, ckrep, kmb[:N_META], vmta[0, :, :N_META],
                             cmrep)
    y_prompt = _merge_mlp(x_prompt.reshape(bsz * seq, d), attn.reshape(bsz * seq, ATTN_W), conv,
                          woa, woc, g1, b1, wup, wdn, g2, b2, _pick_bm(bsz * seq, 512))
    y_prompt = y_prompt.reshape(bsz, seq, d)

    def with_meta(meta_t, body_t):
        rows = body_t.shape[1]
        meta_b = jnp.broadcast_to(meta_t[None, :rows, :N_META], (bsz, rows, N_META))
        return jnp.concatenate([meta_b, body_t], axis=2)

    k_prompt = _positions_minor(with_meta(kmt[0], kt).reshape(1, bsz, HEADS, HEAD_DIM, N_META + seq))
    v_prompt = _positions_minor(with_meta(vmt[0], vt).reshape(1, bsz, HEADS, HEAD_DIM, N_META + seq))
    logf_prompt = _positions_minor(with_meta(lfm.T, lft).reshape(1, bsz, HEADS, N_META + seq))
    conv_prompt = glu.reshape(bsz, seq, CONV_CH)[:, seq - TAIL:, :].reshape(1, bsz, TAIL, CONV_CH)

    qst, kst, vst, lfs, glus = proj(x_sample.reshape(1, dbsz, d), bm=dbsz, sample=True)
    pt_flat = page_table.reshape(-1)
    cache_kt = jnp.transpose(cache_k, (0, 1, 3, 4, 2)).reshape(n_pool, ATTN_W, PAGE)
    cache_vt = jnp.transpose(cache_v, (0, 1, 3, 4, 2)).reshape(n_pool, ATTN_W, PAGE)
    lft_pool = jnp.transpose(cache_logf, (0, 1, 3, 2)).reshape(n_pool, HEADS, PAGE)
    lfn_rep = jnp.broadcast_to(lfs[:, :HEADS, None], (dbsz, HEADS, PAGE))
    attn_st = _paged_attention(pt_flat, qst[0], kst[0], vst[0], lfn_rep, lft_pool, cache_kt, cache_vt, n_pages)
    state_t = jnp.transpose(state_conv[0], (1, 0, 2))
    conv_s = _sample_conv(state_t, glus, cw, cb, cg, cbeta)
    y_sample = _merge_mlp(x_sample.reshape(dbsz, d), attn_st.T.astype(BF16), conv_s,
                          woa, woc, g1, b1, wup, wdn, g2, b2, dbsz)
    y_sample = y_sample.reshape(dbsz, 1, d)

    k_sample = jnp.transpose(kst.reshape(1, 1, HEADS, HEAD_DIM, dbsz), (0, 4, 1, 2, 3))
    v_sample = jnp.transpose(vst.reshape(1, 1, HEADS, HEAD_DIM, dbsz), (0, 4, 1, 2, 3))
    logf_sample = lfs[:, :HEADS].reshape(1, dbsz, 1, HEADS)
    conv_sample = jnp.transpose(jnp.concatenate([state_t[1:], glus[None]], axis=0), (1, 0, 2))[None]

    return (y_prompt, y_sample, k_prompt, v_prompt, logf_prompt, conv_prompt,
            k_sample, v_sample, logf_sample, conv_sample)
```

```python
import functools
import math

import jax
import jax.numpy as jnp
from jax import lax
from jax.experimental import pallas as pl
from jax.experimental.pallas import tpu as pltpu

N_META = 16
HEADS = 8
HEAD_DIM = 64
ATTN_W = HEADS * HEAD_DIM
CONV_CH = 512
CONV_W = 31
TAIL = CONV_W - 1
PAGE = 128
LANES = 128
SUBLANES = 8
BF16_ROWS = 16
ALPHA = 2.0 ** 0.25
LN_EPS = 1e-5
NEG = -1e30
LOG2E = math.log2(math.e)
QSCALE = HEAD_DIM ** -0.5 * LOG2E

F32 = jnp.float32
BF16 = jnp.bfloat16

VMEM_LIMIT = 56 * 1024 * 1024


def _const_spec(shape):
    nd = len(shape)
    return pl.BlockSpec(shape, lambda *_: (0,) * nd, pipeline_mode=pl.Buffered(1))


def _split3(x):
    hi = x.astype(BF16)
    r1 = x - hi.astype(F32)
    mid = r1.astype(BF16)
    lo = (r1 - mid.astype(F32)).astype(BF16)
    return hi, mid, lo


def _dot_exact01(x, m01, *, lhs=True):
    out = None
    for piece in _split3(x):
        if lhs:
            t = jnp.dot(piece, m01, preferred_element_type=F32)
        else:
            t = jnp.dot(m01, piece, preferred_element_type=F32)
        out = t if out is None else out + t
    return out


def _layer_norm(x, g, b):
    mu = jnp.mean(x, axis=-1, keepdims=True)
    xc = x - mu
    var = jnp.mean(xc * xc, axis=-1, keepdims=True)
    return xc * lax.rsqrt(var + LN_EPS) * g + b


def _silu(y):
    return y * (1.0 / (1.0 + jnp.exp(-y)))


V_ROWS = HEAD_DIM + BF16_ROWS
CONV_SUB = 32
CONV_CHAINS = 1
CONV_HALO = 32


def _in_proj_kernel(*refs, sample, conv, tiles_per_seq):
    x_ref, wqkv_ref, wf_ref, wu_ref, wg_ref, bf_ref = refs[:6]
    refs = refs[6:]
    if conv:
        halo_ref, cw_ref, cb_ref, cg_ref, cbeta_ref = refs[:5]
        refs = refs[5:]
        conv_ref, win_ref, sh_ref = refs[-3:]
        refs = refs[:-3]
    bm = x_ref.shape[0]

    conv_steps = []
    if conv:
        g = pl.program_id(0)

        @pl.when(g == 0)
        def _():
            win_ref[...] = jnp.zeros_like(win_ref)

        first_of_seq = ((g - 1) % tiles_per_seq) == 0
        win_ref[pl.ds(0, CONV_HALO), :] = jnp.where(first_of_seq, halo_ref[...], win_ref[pl.ds(0, CONV_HALO), :])
        span = CONV_HALO + bm - SUBLANES
        for r in range(1, SUBLANES):
            sh_ref[r, pl.ds(0, span), :] = win_ref[pl.ds(r, span), :]
        base = CONV_HALO - TAIL

        def conv_piece(s0):
            groups = CONV_SUB // SUBLANES
            accs = [None] * CONV_CHAINS
            for j in range(CONV_W):
                off = s0 + base + j
                r, a = off % SUBLANES, off - off % SUBLANES
                src = win_ref[pl.ds(a, CONV_SUB), :] if r == 0 else sh_ref[r, pl.ds(a, CONV_SUB), :]
                t = cw_ref[j][None] * src.reshape(groups, SUBLANES, CONV_CH)
                c = j % CONV_CHAINS
                accs[c] = t if accs[c] is None else accs[c] + t
            while len(accs) > 1:
                accs = [accs[i] + accs[i + 1] for i in range(0, len(accs), 2)]
            y = _layer_norm(accs[0].reshape(CONV_SUB, CONV_CH) + cb_ref[...], cg_ref[...], cbeta_ref[...])
            conv_ref[pl.ds(s0, CONV_SUB), :] = _silu(y).astype(conv_ref.dtype)

        conv_steps = list(range(0, bm, CONV_SUB))

    def run_conv(n):
        for _ in range(min(n, len(conv_steps))):
            conv_piece(conv_steps.pop(0))

    per_dot = -(-len(conv_steps) // 5) if conv else 0
    x = x_ref[...].astype(BF16)
    dot = lambda w: jnp.dot(x, w, preferred_element_type=F32)

    q = dot(wqkv_ref[:, pl.ds(0, ATTN_W)]) * QSCALE
    if sample:
        qt_ref, kt_ref, vt_ref, lf_ref, glu_ref = refs
        qt_ref[0] = q.T
    else:
        qt_ref, kb_ref, kt_ref, vt_ref, vta_ref, lf_ref, glu_ref = refs
        qt_ref[0] = q.T.astype(BF16)
    run_conv(per_dot)

    kv = dot(wqkv_ref[:, pl.ds(ATTN_W, 2 * ATTN_W)])
    k = kv[:, :ATTN_W]
    kt_ref[0] = k.T
    if not sample:
        kb_ref[...] = k.astype(BF16)
    run_conv(2 * per_dot)

    vt = kv[:, ATTN_W:].T
    vt_ref[0] = vt
    if not sample:
        ones = jnp.ones((BF16_ROWS, bm), BF16)
        for h in range(HEADS):
            vta_ref[0, pl.ds(h * V_ROWS, HEAD_DIM), :] = vt[h * HEAD_DIM:(h + 1) * HEAD_DIM].astype(BF16)
            vta_ref[0, pl.ds(h * V_ROWS + HEAD_DIM, BF16_ROWS), :] = ones
    run_conv(per_dot)

    f = dot(wf_ref[...]) + bf_ref[...]
    lf_ref[...] = jnp.minimum(f, 0.0) - jnp.log(1.0 + jnp.exp(-jnp.abs(f)))
    u = dot(wu_ref[...])
    run_conv(per_dot)
    gate = dot(wg_ref[...])
    run_conv(len(conv_steps))
    glu = u * (1.0 / (1.0 + jnp.exp(-gate)))
    glu_ref[...] = glu
    if conv:
        hist = win_ref[pl.ds(bm, CONV_HALO), :]
        win_ref[pl.ds(0, CONV_HALO), :] = hist
        win_ref[pl.ds(CONV_HALO, bm), :] = glu


def _in_proj(x, wqkv, wf, wu, wg, bf, bm, sample=False, conv_args=None):
    b, t, d = x.shape
    assert t % bm == 0
    nt = t // bm
    tiles = b * nt
    conv = conv_args is not None
    cur = lambda g: jnp.minimum(g, tiles - 1)
    row = lambda w: pl.BlockSpec((bm, w), lambda g: (cur(g), 0))
    col = lambda r: pl.BlockSpec((1, r, bm), lambda g: (cur(g) // nt, 0, cur(g) % nt))
    nat = lambda w, dt: jax.ShapeDtypeStruct((b * t, w), dt)
    tr = lambda r, dt: jax.ShapeDtypeStruct((b, r, t), dt)
    if sample:
        outs = [tr(ATTN_W, F32), tr(ATTN_W, F32), tr(ATTN_W, F32), nat(LANES, F32), nat(CONV_CH, F32)]
        specs = [col(ATTN_W), col(ATTN_W), col(ATTN_W), row(LANES), row(CONV_CH)]
    else:
        outs = [tr(ATTN_W, BF16), nat(ATTN_W, BF16), tr(ATTN_W, F32), tr(ATTN_W, F32),
                tr(HEADS * V_ROWS, BF16), nat(LANES, F32), nat(CONV_CH, F32)]
        specs = [col(ATTN_W), row(ATTN_W), col(ATTN_W), col(ATTN_W), col(HEADS * V_ROWS), row(LANES),
                 row(CONV_CH)]
    args = [x.reshape(b * t, d), wqkv, wf, wu, wg, bf]
    in_specs = [row(d)] + [_const_spec(a.shape) for a in args[1:]]
    scratch = []
    if conv:
        args += list(conv_args)
        in_specs += [_const_spec(a.shape) for a in conv_args]
        outs.append(nat(CONV_CH, BF16))
        specs.append(pl.BlockSpec((bm, CONV_CH), lambda g: (jnp.maximum(g - 1, 0), 0)))
        scratch = [pltpu.VMEM((CONV_HALO + bm, CONV_CH), F32),
                   pltpu.VMEM((SUBLANES, CONV_HALO + bm, CONV_CH), F32)]
    return pl.pallas_call(
        functools.partial(_in_proj_kernel, sample=sample, conv=conv, tiles_per_seq=nt),
        grid=(tiles + 1 if conv else tiles,),
        in_specs=in_specs,
        out_specs=specs,
        out_shape=outs,
        scratch_shapes=scratch,
        compiler_params=pltpu.CompilerParams(dimension_semantics=("arbitrary",),
                                             vmem_limit_bytes=VMEM_LIMIT),
        name="in_proj",
    )(*args)


CS_CHUNK = 256


def _cumsum_kernel(lfm_ref, lf_ref, cm_ref, ck_ref, lft_ref):
    r16 = lax.broadcasted_iota(jnp.int32, (N_META, N_META), 0)
    c16 = lax.broadcasted_iota(jnp.int32, (N_META, N_META), 1)
    tri16 = (c16 <= r16).astype(BF16)
    cm = _dot_exact01(lfm_ref[...], tri16, lhs=False)
    for h in range(HEADS):
        cm_ref[h] = jnp.broadcast_to(cm[:, h:h + 1] * LOG2E, (N_META, LANES))
    carry = cm[N_META - 1:N_META, :]
    rr = lax.broadcasted_iota(jnp.int32, (CS_CHUNK, CS_CHUNK), 0)
    cc = lax.broadcasted_iota(jnp.int32, (CS_CHUNK, CS_CHUNK), 1)
    tri = (cc <= rr).astype(BF16)
    t = lf_ref.shape[1]
    for i in range(t // CS_CHUNK):
        sl = pl.ds(i * CS_CHUNK, CS_CHUNK)
        lf = lf_ref[0, sl, :]
        c = _dot_exact01(lf, tri, lhs=False) + carry
        for h in range(HEADS):
            ck_ref[0, h, sl, :] = jnp.broadcast_to(c[:, h:h + 1] * LOG2E, (CS_CHUNK, LANES))
        lft_ref[0, :, sl] = jnp.transpose(lf)[:HEADS]
        carry = c[CS_CHUNK - 1:CS_CHUNK, :]


def _prompt_cumsum(lf_meta, lf):
    b, t, _ = lf.shape
    return pl.pallas_call(
        _cumsum_kernel,
        grid=(b,),
        in_specs=[_const_spec((N_META, LANES)), pl.BlockSpec((1, t, LANES), lambda i: (i, 0, 0))],
        out_specs=[pl.BlockSpec((HEADS, N_META, LANES), lambda i: (0, 0, 0)),
                   pl.BlockSpec((1, HEADS, t, LANES), lambda i: (i, 0, 0, 0)),
                   pl.BlockSpec((1, HEADS, t), lambda i: (i, 0, 0))],
        out_shape=[jax.ShapeDtypeStruct((HEADS, N_META, LANES), F32),
                   jax.ShapeDtypeStruct((b, HEADS, t, LANES), F32),
                   jax.ShapeDtypeStruct((b, HEADS, t), F32)],
        compiler_params=pltpu.CompilerParams(dimension_semantics=("arbitrary",),
                                             vmem_limit_bytes=VMEM_LIMIT),
        name="prompt_cumsum",
    )(lf_meta, lf)


TQ = 256
TK = 256


def _flash_kernel(qt_ref, kb_ref, vta_ref, ck_ref, km_ref, vmta_ref, cm_ref, o_ref, u_scr):
    qi = pl.program_id(2)
    qt = qt_ref[0].astype(F32)
    feat = lax.broadcasted_iota(jnp.int32, (LANES, TQ), 0)
    reps = TQ // LANES
    qts = [jnp.where((feat // HEAD_DIM) == hh, qt, 0.0).astype(BF16) for hh in range(2)]

    def scores(j):
        off = pl.multiple_of(j * TK, TK)
        kblk = kb_ref[0, pl.ds(off, TK), :]
        return [jnp.dot(kblk, qts[hh], preferred_element_type=F32) for hh in range(2)]

    def stage1(j, s):
        off = pl.multiple_of(j * TK, TK)
        mbs = []
        for hh in range(2):
            u = s[hh] - jnp.concatenate([ck_ref[0, hh, pl.ds(off, TK), :]] * reps, axis=1)
            u_scr[j % 2, hh] = u
            mbs.append(jnp.max(u, axis=0, keepdims=True))
        return mbs

    def stage2(u, m_new, vta):
        p = jnp.exp2(u - m_new).astype(BF16)
        return jnp.dot(vta, p, preferred_element_type=F32)

    def values(j, hh):
        off = pl.multiple_of(j * TK, TK)
        return vta_ref[0, pl.ds(hh * V_ROWS, V_ROWS), pl.ds(off, TK)]

    s0 = scores(0)
    s_meta = [jnp.dot(km_ref[...], qts[hh], preferred_element_type=F32) for hh in range(2)]
    mbs = stage1(0, s0)
    u_meta = [s_meta[hh] - jnp.concatenate([cm_ref[hh]] * reps, axis=1) for hh in range(2)]
    m_meta = [jnp.max(u, axis=0, keepdims=True) for u in u_meta]

    def body(j, carry):
        mbs, state = carry[:2], carry[2:]
        s_next = scores(j + 1)
        out = []
        for hh in range(2):
            m, acc = state[2 * hh], state[2 * hh + 1]
            m_new = jnp.maximum(m, mbs[hh])
            pv = stage2(u_scr[j % 2, hh], m_new, values(j, hh))
            out += [m_new, jnp.exp2(m - m_new) * acc + pv]
        return tuple(stage1(j + 1, s_next)) + tuple(out)

    empty = (jnp.full((1, TQ), NEG, F32), jnp.zeros((V_ROWS, TQ), F32))
    carry = lax.fori_loop(0, qi, body, tuple(mbs) + empty + empty)
    state = carry[2:]

    pv_meta = [stage2(u_meta[hh], m_meta[hh], vmta_ref[pl.ds(hh * V_ROWS, V_ROWS), :]) for hh in range(2)]
    key = lax.broadcasted_iota(jnp.int32, (TK, TQ), 0)
    qry = lax.broadcasted_iota(jnp.int32, (TK, TQ), 1)
    outs = []
    for hh in range(2):
        m, acc = state[2 * hh], state[2 * hh + 1]
        u = jnp.where(key <= qry, u_scr[qi % 2, hh], NEG)
        m_new = jnp.maximum(jnp.maximum(m, jnp.max(u, axis=0, keepdims=True)), m_meta[hh])
        pv = stage2(u, m_new, values(qi, hh))
        acc = jnp.exp2(m - m_new) * acc + pv + jnp.exp2(m_meta[hh] - m_new) * pv_meta[hh]
        outs.append(acc[:HEAD_DIM] / acc[HEAD_DIM:HEAD_DIM + 1])
    o_ref[0] = jnp.concatenate(outs, axis=0).T.astype(o_ref.dtype)


def _prompt_attention(qt, kb, vta, ckrep, kmb, vmta, cmrep):
    b, _, t = qt.shape
    assert t % TQ == 0 and TQ == TK
    grid = (b, ATTN_W // LANES, t // TQ)
    return pl.pallas_call(
        _flash_kernel,
        grid=grid,
        in_specs=[
            pl.BlockSpec((1, LANES, TQ), lambda i, p, j: (i, p, j)),
            pl.BlockSpec((1, t, LANES), lambda i, p, j: (i, 0, p)),
            pl.BlockSpec((1, 2 * V_ROWS, t), lambda i, p, j: (i, p, 0)),
            pl.BlockSpec((1, 2, t, LANES), lambda i, p, j: (i, p, 0, 0)),
            pl.BlockSpec((N_META, LANES), lambda i, p, j: (0, p)),
            pl.BlockSpec((2 * V_ROWS, N_META), lambda i, p, j: (p, 0)),
            pl.BlockSpec((2, N_META, LANES), lambda i, p, j: (p, 0, 0)),
        ],
        out_specs=pl.BlockSpec((1, TQ, LANES), lambda i, p, j: (i, j, p)),
        out_shape=jax.ShapeDtypeStruct((b, t, ATTN_W), BF16),
        scratch_shapes=[pltpu.VMEM((2, 2, TK, TQ), F32)],
        compiler_params=pltpu.CompilerParams(dimension_semantics=("arbitrary",) * 3,
                                             vmem_limit_bytes=VMEM_LIMIT),
        name="prompt_attention",
    )(qt, kb, vta, ckrep, kmb, vmta, cmrep)


def _sample_conv_kernel(st_ref, glu_ref, w_ref, cb_ref, g_ref, b_ref, o_ref):
    w = w_ref[...]
    acc = w[TAIL:TAIL + 1, :] * glu_ref[...]
    for j in range(TAIL):
        acc = acc + w[j:j + 1, :] * st_ref[j]
    y = _layer_norm(acc + cb_ref[...], g_ref[...], b_ref[...])
    o_ref[...] = _silu(y).astype(o_ref.dtype)


def _sample_conv(state_t, glus, conv_w, conv_b, g, b):
    n = glus.shape[0]
    return pl.pallas_call(
        _sample_conv_kernel,
        out_shape=jax.ShapeDtypeStruct((n, CONV_CH), BF16),
        compiler_params=pltpu.CompilerParams(vmem_limit_bytes=VMEM_LIMIT),
        name="sample_conv",
    )(state_t, glus, conv_w, conv_b, g, b)


FF_CHUNK = 1024


def _mlp_kernel(x_ref, a_ref, c_ref, woa_ref, woc_ref, g1_ref, b1_ref, wup_ref, wdn_ref, g2_ref, b2_ref,
                o_ref):
    mixed = jnp.dot(a_ref[...], woa_ref[...], preferred_element_type=F32)
    mixed = mixed + jnp.dot(c_ref[...], woc_ref[...], preferred_element_type=F32)
    h = _layer_norm(ALPHA * x_ref[...] + mixed, g1_ref[...], b1_ref[...])
    hb = h.astype(BF16)
    d_ff = wup_ref.shape[1]
    m = None
    for c0 in range(0, d_ff, FF_CHUNK):
        u = jnp.dot(hb, wup_ref[:, pl.ds(c0, FF_CHUNK)], preferred_element_type=F32)
        u = jnp.maximum(u, 0.0)
        u = (u * u).astype(BF16)
        t = jnp.dot(u, wdn_ref[pl.ds(c0, FF_CHUNK), :], preferred_element_type=F32)
        m = t if m is None else m + t
    o_ref[...] = _layer_norm(ALPHA * h + m, g2_ref[...], b2_ref[...])


def _merge_mlp(x, attn, conv, woa, woc, g1, b1, wup, wdn, g2, b2, bm):
    m, d = x.shape
    assert m % bm == 0
    row = lambda w: pl.BlockSpec((bm, w), lambda i: (i, 0))
    consts = [woa, woc, g1, b1, wup, wdn, g2, b2]
    return pl.pallas_call(
        _mlp_kernel,
        grid=(m // bm,),
        in_specs=[row(d), row(ATTN_W), row(CONV_CH)] + [_const_spec(a.shape) for a in consts],
        out_specs=row(d),
        out_shape=jax.ShapeDtypeStruct((m, d), F32),
        compiler_params=pltpu.CompilerParams(dimension_semantics=("arbitrary",),
                                             vmem_limit_bytes=VMEM_LIMIT),
        name="merge_mlp",
    )(x, attn, conv, *consts)


NBUF = 32
PGROUP = 4


def _per_head(col):
    return jnp.broadcast_to(col[:, None, :], (HEADS, HEAD_DIM, 1)).reshape(ATTN_W, 1)


def _paged_attn_kernel(pt_ref, qt_ref, knt_ref, vnt_ref, lfn_ref, lft_hbm, ck_hbm, cv_hbm, o_ref,
                       buf, sem, lfbuf, lfsem, d_scr, s_all, stat_ref, qb_ref, acc_ref, ctr, first_ref):
    t = pl.program_id(0)
    n = first_ref.shape[0]
    n_pages = s_all.shape[1]
    nbuf = buf.shape[0]
    rows = n_pages * HEADS
    ISSUED, CONSUMED, I_TRIP, I_VALUES, I_OFF = range(5)

    def issue_next():
        it, in_v, off = ctr[I_TRIP], ctr[I_VALUES], ctr[I_OFF]
        slot = ctr[ISSUED] % nbuf

        @pl.when((it <= n) & (in_v == 0))
        def _():
            pltpu.make_async_copy(ck_hbm.at[pt_ref[it * n_pages + off]], buf.at[slot], sem.at[slot]).start()

        @pl.when((it <= n) & (in_v == 1))
        def _():
            pltpu.make_async_copy(cv_hbm.at[pt_ref[(it - 1) * n_pages + off]], buf.at[slot],
                                  sem.at[slot]).start()

        ctr[ISSUED] = ctr[ISSUED] + 1
        seg_done = off + 1 == n_pages
        to_values = seg_done & (in_v == 0) & (it >= 1)
        to_next = seg_done & ((in_v == 1) | (it == 0))
        nxt = it + 1
        nxt_values = nxt >= n
        ctr[I_TRIP] = jnp.where(to_next, nxt, it)
        ctr[I_VALUES] = jnp.where(to_values, 1, jnp.where(to_next, nxt_values.astype(jnp.int32), in_v))
        first_cur = first_ref[jnp.clip(it - 1, 0, n - 1)]
        first_nxt = first_ref[jnp.minimum(it, n - 1)]
        ctr[I_OFF] = jnp.where(to_values, first_cur,
                               jnp.where(to_next, jnp.where(nxt_values, first_nxt, 0), off + 1))

    def consume():
        slot = ctr[CONSUMED] % nbuf
        pltpu.make_async_copy(ck_hbm.at[0], buf.at[slot], sem.at[slot]).wait()
        ctr[CONSUMED] = ctr[CONSUMED] + 1
        return slot

    def lf_copy(seq, slot, p):
        return pltpu.make_async_copy(lft_hbm.at[pt_ref[seq * n_pages + p]], lfbuf.at[slot, p], lfsem.at[slot])

    def lf_start(seq, slot):
        def body(p, _):
            lf_copy(seq, slot, p).start()
            return 0
        lax.fori_loop(0, n_pages, body, 0)

    @pl.when(t == 0)
    def _():
        for i in range(5):
            ctr[i] = 0

        def clear(i, _):
            first_ref[i] = 0
            return 0
        lax.fori_loop(0, n, clear, 0)
        for _ in range(nbuf):
            issue_next()
        lf_start(0, 0)

    @pl.when(t + 1 < n)
    def _():
        lf_start(t + 1, (t + 1) % 2)

    seq_lane = lax.broadcasted_iota(jnp.int32, qt_ref.shape, 1)
    pick = lambda ref, seq: jnp.sum(jnp.where(seq_lane == seq, ref[...], 0.0), axis=1, keepdims=True)

    @pl.when(t < n)
    def _():
        _paged_keys(t, n_pages, rows, pick, consume, issue_next, lf_copy,
                    qt_ref, knt_ref, lfn_ref, buf, lfbuf, d_scr, s_all, stat_ref, qb_ref, first_ref)

    @pl.when(t >= 1)
    def _():
        _paged_values(t - 1, n_pages, pick, consume, issue_next, vnt_ref, buf, s_all, stat_ref, acc_ref,
                      first_ref, o_ref)


def _paged_keys(b, n_pages, rows, pick, consume, issue_next, lf_copy,
                qt_ref, knt_ref, lfn_ref, buf, lfbuf, d_scr, s_all, stat_ref, qb_ref, first_ref):
    par = b % 2

    def lf_wait(p, _):
        lf_copy(b, par, p).wait()
        return 0
    lax.fori_loop(0, n_pages, lf_wait, 0)

    lf = lfbuf[par].reshape(rows, PAGE)
    s_in = lax.broadcasted_iota(jnp.int32, (PAGE, PAGE), 0)
    s_out = lax.broadcasted_iota(jnp.int32, (PAGE, PAGE), 1)
    later = (s_in > s_out).astype(BF16)
    local = _dot_exact01(lf, later, lhs=True)
    tot = _dot_exact01(lf, jnp.ones((PAGE, PAGE), BF16), lhs=True)
    r_out = lax.broadcasted_iota(jnp.int32, (rows, rows), 0)
    r_in = lax.broadcasted_iota(jnp.int32, (rows, rows), 1)
    later_pages = ((r_in % HEADS == r_out % HEADS) & (r_in // HEADS > r_out // HEADS)).astype(BF16)
    cross = _dot_exact01(tot, later_pages, lhs=False)
    d = (local + cross).reshape(n_pages, HEADS, PAGE) + lfn_ref[0][None]
    d_scr[...] = d.reshape(rows, PAGE) * LOG2E

    qcol, kcol = pick(qt_ref, b), pick(knt_ref, b)
    qb_ref[...] = jnp.broadcast_to(qcol, (ATTN_W, PAGE))
    s_new = jnp.sum((qcol * kcol).reshape(HEADS, HEAD_DIM, 1), axis=1)

    def key_group(g, _):
        for i in range(PGROUP):
            u = g * PGROUP + i
            slot = consume()
            prod = buf[slot] * qb_ref[...]
            s = jnp.sum(prod.reshape(HEADS, HEAD_DIM, PAGE), axis=1)
            s_all[par, u] = s + d_scr[pl.ds(pl.multiple_of(u * HEADS, HEADS), HEADS), :]
            issue_next()
        return 0

    lax.fori_loop(0, n_pages // PGROUP, key_group, 0)

    s = s_all[par]
    m = jnp.max(jnp.max(s, axis=0), axis=1, keepdims=True)
    m = jnp.maximum(m, s_new)
    p = jnp.exp2(s - m[None])
    p_new = jnp.exp2(s_new - m)
    l = jnp.sum(jnp.sum(p, axis=0), axis=1, keepdims=True) + p_new
    s_all[par] = p
    stat_ref[par, 0] = jnp.broadcast_to(l, (HEADS, PAGE))
    stat_ref[par, 1] = jnp.broadcast_to(p_new, (HEADS, PAGE))

    page_max = jnp.max(jnp.max(p, axis=1), axis=1, keepdims=True)
    page_idx = lax.broadcasted_iota(jnp.int32, (n_pages, 1), 0)
    first = jnp.min(jnp.where(page_max > 0.0, page_idx, n_pages - 1))
    last = b == first_ref.shape[0] - 1
    first_ref[b] = jnp.where(last, 0, (first // PGROUP) * PGROUP)


def _paged_values(b, n_pages, pick, consume, issue_next, vnt_ref, buf, s_all, stat_ref, acc_ref, first_ref,
                  o_ref):
    par = b % 2
    vcol = pick(vnt_ref, b)
    l = stat_ref[par, 0][:, :1]
    p_new = stat_ref[par, 1][:, :1]
    pos = lax.broadcasted_iota(jnp.int32, (ATTN_W, PAGE), 1)
    acc_ref[...] = jnp.where(pos == 0, _per_head(p_new) * vcol, 0.0)

    def value_group(g, _):
        part = None
        for i in range(PGROUP):
            u = g * PGROUP + i
            slot = consume()
            pe = jnp.broadcast_to(s_all[par, u][:, None, :], (HEADS, HEAD_DIM, PAGE)).reshape(ATTN_W, PAGE)
            t = buf[slot] * pe
            part = t if part is None else part + t
            issue_next()
        acc_ref[...] += part
        return 0

    lax.fori_loop(first_ref[b] // PGROUP, n_pages // PGROUP, value_group, 0)
    ocol = jnp.sum(acc_ref[...], axis=1, keepdims=True) / _per_head(l)

    @pl.when(b == 0)
    def _():
        o_ref[...] = jnp.zeros_like(o_ref)

    seq_lane = lax.broadcasted_iota(jnp.int32, o_ref.shape, 1)
    o_ref[...] = jnp.where(seq_lane == b, ocol, o_ref[...])


def _paged_attention(pt_flat, qt, knt, vnt, lfn_rep, lft_pool, cache_kt, cache_vt, n_pages):
    n = qt.shape[1]
    nbuf = min(NBUF, n_pages // 2)
    assert n_pages % PGROUP == 0 and nbuf >= 1
    rows = n_pages * HEADS
    full = pl.BlockSpec((ATTN_W, n), lambda i, pt: (0, 0))
    hbm = pl.BlockSpec(memory_space=pl.ANY)
    grid_spec = pltpu.PrefetchScalarGridSpec(
        num_scalar_prefetch=1,
        grid=(n + 1,),
        in_specs=[full, full, full,
                  pl.BlockSpec((1, HEADS, PAGE), lambda i, pt: (jnp.minimum(i, n - 1), 0, 0)), hbm, hbm, hbm],
        out_specs=full,
        scratch_shapes=[pltpu.VMEM((nbuf, ATTN_W, PAGE), F32), pltpu.SemaphoreType.DMA((nbuf,)),
                        pltpu.VMEM((2, n_pages, HEADS, PAGE), F32), pltpu.SemaphoreType.DMA((2,)),
                        pltpu.VMEM((rows, PAGE), F32),
                        pltpu.VMEM((2, n_pages, HEADS, PAGE), F32),
                        pltpu.VMEM((2, 2, HEADS, PAGE), F32),
                        pltpu.VMEM((ATTN_W, PAGE), F32), pltpu.VMEM((ATTN_W, PAGE), F32),
                        pltpu.SMEM((8,), jnp.int32), pltpu.SMEM((n,), jnp.int32)],
    )
    return pl.pallas_call(
        _paged_attn_kernel,
        grid_spec=grid_spec,
        out_shape=jax.ShapeDtypeStruct((ATTN_W, n), F32),
        compiler_params=pltpu.CompilerParams(dimension_semantics=("arbitrary",),
                                             vmem_limit_bytes=VMEM_LIMIT),
        name="paged_attention",
    )(pt_flat, qt, knt, vnt, lfn_rep, lft_pool, cache_kt, cache_vt)


def _pick_bm(m, target):
    bm = min(m, target)
    while m % bm:
        bm //= 2
    return bm


def _positions_minor(x):
    nd = x.ndim
    return jnp.transpose(x, (0, 1, nd - 1) + tuple(range(2, nd - 1)))


def kernel(x_prompt, x_sample, cache_k, cache_v, cache_logf, state_conv, page_table, meta_tokens, w_in, b_f,
           conv_w, conv_b, conv_ln_g, conv_ln_b, w_out, ln1_g, ln1_b, w_up, w_down, ln2_g, ln2_b):
    bsz, seq, d = x_prompt.shape
    dbsz, dseq, _ = x_sample.shape
    depth = w_in.shape[0]
    assert depth == 1 and dseq == 1
    n_pool = cache_k.shape[1]
    n_pages = page_table.shape[1]

    w = w_in[0]
    o_f = 3 * ATTN_W
    wqkv = w[:, :o_f].astype(BF16)
    wf = jnp.pad(w[:, o_f:o_f + HEADS], ((0, 0), (0, LANES - HEADS))).astype(BF16)
    wu = w[:, o_f + HEADS:o_f + HEADS + CONV_CH].astype(BF16)
    wg = w[:, o_f + HEADS + CONV_CH:].astype(BF16)
    bf = jnp.pad(b_f[0], (0, LANES - HEADS)).reshape(1, LANES)
    woa = w_out[0, :ATTN_W].astype(BF16)
    woc = w_out[0, ATTN_W:].astype(BF16)
    wup = w_up[0].astype(BF16)
    wdn = w_down[0].astype(BF16)
    row = lambda a: a[0].reshape(1, -1)
    cw, cb, cg, cbeta = conv_w[0], row(conv_b), row(conv_ln_g), row(conv_ln_b)
    cw_rep = jnp.broadcast_to(cw[:, None, :], (CONV_W, SUBLANES, CONV_CH))
    g1, b1, g2, b2 = row(ln1_g), row(ln1_b), row(ln2_g), row(ln2_b)
    proj = functools.partial(_in_proj, wqkv=wqkv, wf=wf, wu=wu, wg=wg, bf=bf)

    meta_pad = jnp.pad(meta_tokens, ((0, LANES - N_META), (0, 0)))[None]
    _, kmb, kmt, vmt, vmta, lfm, glum = proj(meta_pad, bm=LANES)
    lfm = lfm[:N_META]
    meta_halo = jnp.concatenate([jnp.zeros((CONV_HALO - N_META, CONV_CH), F32), glum[:N_META]], axis=0)
    qt, kb, kt, vt, vta, lf, glu, conv = proj(x_prompt, bm=_pick_bm(seq, 512),
                                              conv_args=(meta_halo, cw_rep, cb, cg, cbeta))
    cmrep, ckrep, lft = _prompt_cumsum(lfm, lf.reshape(bsz, seq, LANES))
    attn = _prompt_attention(qt, kb.reshape(bsz, seq, ATTN_W), vta, ckrep, kmb[:N_META], vmta[0, :, :N_META],
                             cmrep)
    y_prompt = _merge_mlp(x_prompt.reshape(bsz * seq, d), attn.reshape(bsz * seq, ATTN_W), conv,
                          woa, woc, g1, b1, wup, wdn, g2, b2, _pick_bm(bsz * seq, 512))
    y_prompt = y_prompt.reshape(bsz, seq, d)

    def with_meta(meta_t, body_t):
        rows = body_t.shape[1]
        meta_b = jnp.broadcast_to(meta_t[None, :rows, :N_META], (bsz, rows, N_META))
        return jnp.concatenate([meta_b, body_t], axis=2)

    k_prompt = _positions_minor(with_meta(kmt[0], kt).reshape(1, bsz, HEADS, HEAD_DIM, N_META + seq))
    v_prompt = _positions_minor(with_meta(vmt[0], vt).reshape(1, bsz, HEADS, HEAD_DIM, N_META + seq))
    logf_prompt = _positions_minor(with_meta(lfm.T, lft).reshape(1, bsz, HEADS, N_META + seq))
    conv_prompt = glu.reshape(bsz, seq, CONV_CH)[:, seq - TAIL:, :].reshape(1, bsz, TAIL, CONV_CH)

    qst, kst, vst, lfs, glus = proj(x_sample.reshape(1, dbsz, d), bm=dbsz, sample=True)
    pt_flat = page_table.reshape(-1)
    cache_kt = jnp.transpose(cache_k, (0, 1, 3, 4, 2)).reshape(n_pool, ATTN_W, PAGE)
    cache_vt = jnp.transpose(cache_v, (0, 1, 3, 4, 2)).reshape(n_pool, ATTN_W, PAGE)
    lft_pool = jnp.transpose(cache_logf, (0, 1, 3, 2)).reshape(n_pool, HEADS, PAGE)
    lfn_rep = jnp.broadcast_to(lfs[:, :HEADS, None], (dbsz, HEADS, PAGE))
    attn_st = _paged_attention(pt_flat, qst[0], kst[0], vst[0], lfn_rep, lft_pool, cache_kt, cache_vt, n_pages)
    state_t = jnp.transpose(state_conv[0], (1, 0, 2))
    conv_s = _sample_conv(state_t, glus, cw, cb, cg, cbeta)
    y_sample = _merge_mlp(x_sample.reshape(dbsz, d), attn_st.T.astype(BF16), conv_s,
                          woa, woc, g1, b1, wup, wdn, g2, b2, dbsz)
    y_sample = y_sample.reshape(dbsz, 1, d)

    k_sample = jnp.transpose(kst.reshape(1, 1, HEADS, HEAD_DIM, dbsz), (0, 4, 1, 2, 3))
    v_sample = jnp.transpose(vst.reshape(1, 1, HEADS, HEAD_DIM, dbsz), (0, 4, 1, 2, 3))
    logf_sample = lfs[:, :HEADS].reshape(1, dbsz, 1, HEADS)
    conv_sample = jnp.transpose(jnp.concatenate([state_t[1:], glus[None]], axis=0), (1, 0, 2))[None]

    return (y_prompt, y_sample, k_prompt, v_prompt, logf_prompt, conv_prompt,
            k_sample, v_sample, logf_sample, conv_sample)
```

```python
import functools
import math

import jax
import jax.numpy as jnp
from jax import lax
from jax.experimental import pallas as pl
from jax.experimental.pallas import tpu as pltpu

N_META = 16
HEADS = 8
HEAD_DIM = 64
ATTN_W = HEADS * HEAD_DIM
CONV_CH = 512
CONV_W = 31
TAIL = CONV_W - 1
PAGE = 128
LANES = 128
SUBLANES = 8
BF16_ROWS = 16
ALPHA = 2.0 ** 0.25
LN_EPS = 1e-5
NEG = -1e30
LOG2E = math.log2(math.e)
QSCALE = HEAD_DIM ** -0.5 * LOG2E

F32 = jnp.float32
BF16 = jnp.bfloat16

VMEM_LIMIT = 56 * 1024 * 1024


def _const_spec(shape):
    nd = len(shape)
    return pl.BlockSpec(shape, lambda *_: (0,) * nd, pipeline_mode=pl.Buffered(1))


def _split3(x):
    hi = x.astype(BF16)
    r1 = x - hi.astype(F32)
    mid = r1.astype(BF16)
    lo = (r1 - mid.astype(F32)).astype(BF16)
    return hi, mid, lo


def _dot_exact01(x, m01, *, lhs=True):
    out = None
    for piece in _split3(x):
        if lhs:
            t = jnp.dot(piece, m01, preferred_element_type=F32)
        else:
            t = jnp.dot(m01, piece, preferred_element_type=F32)
        out = t if out is None else out + t
    return out


def _layer_norm(x, g, b):
    mu = jnp.mean(x, axis=-1, keepdims=True)
    xc = x - mu
    var = jnp.mean(xc * xc, axis=-1, keepdims=True)
    return xc * lax.rsqrt(var + LN_EPS) * g + b


def _silu(y):
    return y * (1.0 / (1.0 + jnp.exp(-y)))


V_ROWS = HEAD_DIM + BF16_ROWS
CONV_SUB = 32
CONV_CHAINS = 1
CONV_HALO = 32


def _in_proj_kernel(*refs, sample, conv, tiles_per_seq):
    x_ref, wqkv_ref, wf_ref, wu_ref, wg_ref, bf_ref = refs[:6]
    refs = refs[6:]
    if conv:
        halo_ref, cw_ref, cb_ref, cg_ref, cbeta_ref = refs[:5]
        refs = refs[5:]
        conv_ref, win_ref, sh_ref = refs[-3:]
        refs = refs[:-3]
    bm = x_ref.shape[0]

    conv_steps = []
    if conv:
        g = pl.program_id(0)

        @pl.when(g == 0)
        def _():
            win_ref[...] = jnp.zeros_like(win_ref)

        first_of_seq = ((g - 1) % tiles_per_seq) == 0
        win_ref[pl.ds(0, CONV_HALO), :] = jnp.where(first_of_seq, halo_ref[...], win_ref[pl.ds(0, CONV_HALO), :])
        span = CONV_HALO + bm - SUBLANES
        for r in range(1, SUBLANES):
            sh_ref[r, pl.ds(0, span), :] = win_ref[pl.ds(r, span), :]
        base = CONV_HALO - TAIL

        def conv_piece(s0):
            groups = CONV_SUB // SUBLANES
            accs = [None] * CONV_CHAINS
            for j in range(CONV_W):
                off = s0 + base + j
                r, a = off % SUBLANES, off - off % SUBLANES
                src = win_ref[pl.ds(a, CONV_SUB), :] if r == 0 else sh_ref[r, pl.ds(a, CONV_SUB), :]
                t = cw_ref[j][None] * src.reshape(groups, SUBLANES, CONV_CH)
                c = j % CONV_CHAINS
                accs[c] = t if accs[c] is None else accs[c] + t
            while len(accs) > 1:
                accs = [accs[i] + accs[i + 1] for i in range(0, len(accs), 2)]
            y = _layer_norm(accs[0].reshape(CONV_SUB, CONV_CH) + cb_ref[...], cg_ref[...], cbeta_ref[...])
            conv_ref[pl.ds(s0, CONV_SUB), :] = _silu(y).astype(conv_ref.dtype)

        conv_steps = list(range(0, bm, CONV_SUB))

    def run_conv(n):
        for _ in range(min(n, len(conv_steps))):
            conv_piece(conv_steps.pop(0))

    per_dot = -(-len(conv_steps) // 5) if conv else 0
    x = x_ref[...].astype(BF16)
    dot = lambda w: jnp.dot(x, w, preferred_element_type=F32)

    q = dot(wqkv_ref[:, pl.ds(0, ATTN_W)]) * QSCALE
    if sample:
        qt_ref, kt_ref, vt_ref, lf_ref, glu_ref = refs
        qt_ref[0] = q.T
    else:
        qt_ref, kb_ref, kt_ref, vt_ref, vta_ref, lf_ref, glu_ref = refs
        qt_ref[0] = q.T.astype(BF16)
    run_conv(per_dot)

    kv = dot(wqkv_ref[:, pl.ds(ATTN_W, 2 * ATTN_W)])
    k = kv[:, :ATTN_W]
    kt_ref[0] = k.T
    if not sample:
        kb_ref[...] = k.astype(BF16)
    run_conv(2 * per_dot)

    vt = kv[:, ATTN_W:].T
    vt_ref[0] = vt
    if not sample:
        ones = jnp.ones((BF16_ROWS, bm), BF16)
        for h in range(HEADS):
            vta_ref[0, pl.ds(h * V_ROWS, HEAD_DIM), :] = vt[h * HEAD_DIM:(h + 1) * HEAD_DIM].astype(BF16)
            vta_ref[0, pl.ds(h * V_ROWS + HEAD_DIM, BF16_ROWS), :] = ones
    run_conv(per_dot)

    f = dot(wf_ref[...]) + bf_ref[...]
    lf_ref[...] = jnp.minimum(f, 0.0) - jnp.log(1.0 + jnp.exp(-jnp.abs(f)))
    u = dot(wu_ref[...])
    run_conv(per_dot)
    gate = dot(wg_ref[...])
    run_conv(len(conv_steps))
    glu = u * (1.0 / (1.0 + jnp.exp(-gate)))
    glu_ref[...] = glu
    if conv:
        hist = win_ref[pl.ds(bm, CONV_HALO), :]
        win_ref[pl.ds(0, CONV_HALO), :] = hist
        win_ref[pl.ds(CONV_HALO, bm), :] = glu


def _in_proj(x, wqkv, wf, wu, wg, bf, bm, sample=False, conv_args=None):
    b, t, d = x.shape
    assert t % bm == 0
    nt = t // bm
    tiles = b * nt
    conv = conv_args is not None
    cur = lambda g: jnp.minimum(g, tiles - 1)
    row = lambda w: pl.BlockSpec((bm, w), lambda g: (cur(g), 0))
    col = lambda r: pl.BlockSpec((1, r, bm), lambda g: (cur(g) // nt, 0, cur(g) % nt))
    nat = lambda w, dt: jax.ShapeDtypeStruct((b * t, w), dt)
    tr = lambda r, dt: jax.ShapeDtypeStruct((b, r, t), dt)
    if sample:
        outs = [tr(ATTN_W, F32), tr(ATTN_W, F32), tr(ATTN_W, F32), nat(LANES, F32), nat(CONV_CH, F32)]
        specs = [col(ATTN_W), col(ATTN_W), col(ATTN_W), row(LANES), row(CONV_CH)]
    else:
        outs = [tr(ATTN_W, BF16), nat(ATTN_W, BF16), tr(ATTN_W, F32), tr(ATTN_W, F32),
                tr(HEADS * V_ROWS, BF16), nat(LANES, F32), nat(CONV_CH, F32)]
        specs = [col(ATTN_W), row(ATTN_W), col(ATTN_W), col(ATTN_W), col(HEADS * V_ROWS), row(LANES),
                 row(CONV_CH)]
    args = [x.reshape(b * t, d), wqkv, wf, wu, wg, bf]
    in_specs = [row(d)] + [_const_spec(a.shape) for a in args[1:]]
    scratch = []
    if conv:
        args += list(conv_args)
        in_specs += [_const_spec(a.shape) for a in conv_args]
        outs.append(nat(CONV_CH, BF16))
        specs.append(pl.BlockSpec((bm, CONV_CH), lambda g: (jnp.maximum(g - 1, 0), 0)))
        scratch = [pltpu.VMEM((CONV_HALO + bm, CONV_CH), F32),
                   pltpu.VMEM((SUBLANES, CONV_HALO + bm, CONV_CH), F32)]
    return pl.pallas_call(
        functools.partial(_in_proj_kernel, sample=sample, conv=conv, tiles_per_seq=nt),
        grid=(tiles + 1 if conv else tiles,),
        in_specs=in_specs,
        out_specs=specs,
        out_shape=outs,
        scratch_shapes=scratch,
        compiler_params=pltpu.CompilerParams(dimension_semantics=("arbitrary",),
                                             vmem_limit_bytes=VMEM_LIMIT),
        name="in_proj",
    )(*args)


CS_CHUNK = 256


def _cumsum_kernel(lfm_ref, lf_ref, cm_ref, ck_ref, lft_ref):
    r16 = lax.broadcasted_iota(jnp.int32, (N_META, N_META), 0)
    c16 = lax.broadcasted_iota(jnp.int32, (N_META, N_META), 1)
    tri16 = (c16 <= r16).astype(BF16)
    cm = _dot_exact01(lfm_ref[...], tri16, lhs=False)
    for h in range(HEADS):
        cm_ref[h] = jnp.broadcast_to(cm[:, h:h + 1] * LOG2E, (N_META, LANES))
    carry = cm[N_META - 1:N_META, :]
    rr = lax.broadcasted_iota(jnp.int32, (CS_CHUNK, CS_CHUNK), 0)
    cc = lax.broadcasted_iota(jnp.int32, (CS_CHUNK, CS_CHUNK), 1)
    tri = (cc <= rr).astype(BF16)
    t = lf_ref.shape[1]
    for i in range(t // CS_CHUNK):
        sl = pl.ds(i * CS_CHUNK, CS_CHUNK)
        lf = lf_ref[0, sl, :]
        c = _dot_exact01(lf, tri, lhs=False) + carry
        for h in range(HEADS):
            ck_ref[0, h, sl, :] = jnp.broadcast_to(c[:, h:h + 1] * LOG2E, (CS_CHUNK, LANES))
        lft_ref[0, :, sl] = jnp.transpose(lf)[:HEADS]
        carry = c[CS_CHUNK - 1:CS_CHUNK, :]


def _prompt_cumsum(lf_meta, lf):
    b, t, _ = lf.shape
    return pl.pallas_call(
        _cumsum_kernel,
        grid=(b,),
        in_specs=[_const_spec((N_META, LANES)), pl.BlockSpec((1, t, LANES), lambda i: (i, 0, 0))],
        out_specs=[pl.BlockSpec((HEADS, N_META, LANES), lambda i: (0, 0, 0)),
                   pl.BlockSpec((1, HEADS, t, LANES), lambda i: (i, 0, 0, 0)),
                   pl.BlockSpec((1, HEADS, t), lambda i: (i, 0, 0))],
        out_shape=[jax.ShapeDtypeStruct((HEADS, N_META, LANES), F32),
                   jax.ShapeDtypeStruct((b, HEADS, t, LANES), F32),
                   jax.ShapeDtypeStruct((b, HEADS, t), F32)],
        compiler_params=pltpu.CompilerParams(dimension_semantics=("arbitrary",),
                                             vmem_limit_bytes=VMEM_LIMIT),
        name="prompt_cumsum",
    )(lf_meta, lf)


TQ = 512
TK = 256


def _flash_kernel(qt_ref, kb_ref, vta_ref, ck_ref, km_ref, vmta_ref, cm_ref, o_ref, u_scr):
    qi = pl.program_id(2)
    qt = qt_ref[0].astype(F32)
    feat = lax.broadcasted_iota(jnp.int32, (LANES, TQ), 0)
    reps = TQ // LANES
    qts = [jnp.where((feat // HEAD_DIM) == hh, qt, 0.0).astype(BF16) for hh in range(2)]

    def scores(j):
        off = pl.multiple_of(j * TK, TK)
        kblk = kb_ref[0, pl.ds(off, TK), :]
        return [jnp.dot(kblk, qts[hh], preferred_element_type=F32) for hh in range(2)]

    def stage1(j, s):
        off = pl.multiple_of(j * TK, TK)
        mbs = []
        for hh in range(2):
            u = s[hh] - jnp.concatenate([ck_ref[0, hh, pl.ds(off, TK), :]] * reps, axis=1)
            u_scr[j % 2, hh] = u
            mbs.append(jnp.max(u, axis=0, keepdims=True))
        return mbs

    def stage2(u, m_new, vta):
        p = jnp.exp2(u - m_new).astype(BF16)
        return jnp.dot(vta, p, preferred_element_type=F32)

    def values(j, hh):
        off = pl.multiple_of(j * TK, TK)
        return vta_ref[0, pl.ds(hh * V_ROWS, V_ROWS), pl.ds(off, TK)]

    s0 = scores(0)
    s_meta = [jnp.dot(km_ref[...], qts[hh], preferred_element_type=F32) for hh in range(2)]
    mbs = stage1(0, s0)
    u_meta = [s_meta[hh] - jnp.concatenate([cm_ref[hh]] * reps, axis=1) for hh in range(2)]
    m_meta = [jnp.max(u, axis=0, keepdims=True) for u in u_meta]

    def body(j, carry):
        mbs, state = carry[:2], carry[2:]
        s_next = scores(j + 1)
        out = []
        for hh in range(2):
            m, acc = state[2 * hh], state[2 * hh + 1]
            m_new = jnp.maximum(m, mbs[hh])
            pv = stage2(u_scr[j % 2, hh], m_new, values(j, hh))
            out += [m_new, jnp.exp2(m - m_new) * acc + pv]
        return tuple(stage1(j + 1, s_next)) + tuple(out)

    n_diag = TQ // TK
    n_full = qi * n_diag
    empty = (jnp.full((1, TQ), NEG, F32), jnp.zeros((V_ROWS, TQ), F32))
    carry = lax.fori_loop(0, n_full, body, tuple(mbs) + empty + empty)
    state = list(carry[2:])

    s_diag = [scores(n_full + d) for d in range(1, n_diag)]
    pv_meta = [stage2(u_meta[hh], m_meta[hh], vmta_ref[pl.ds(hh * V_ROWS, V_ROWS), :]) for hh in range(2)]
    key = lax.broadcasted_iota(jnp.int32, (TK, TQ), 0)
    qry = lax.broadcasted_iota(jnp.int32, (TK, TQ), 1)
    for d in range(n_diag):
        off = pl.multiple_of((n_full + d) * TK, TK)
        for hh in range(2):
            m, acc = state[2 * hh], state[2 * hh + 1]
            if d == 0:
                u = u_scr[n_full % 2, hh]
            else:
                u = s_diag[d - 1][hh] - jnp.concatenate([ck_ref[0, hh, pl.ds(off, TK), :]] * reps, axis=1)
            u = jnp.where(key + d * TK <= qry, u, NEG)
            m_new = jnp.maximum(m, jnp.max(u, axis=0, keepdims=True))
            pv = stage2(u, m_new, values(n_full + d, hh))
            state[2 * hh], state[2 * hh + 1] = m_new, jnp.exp2(m - m_new) * acc + pv
    outs = []
    for hh in range(2):
        m, acc = state[2 * hh], state[2 * hh + 1]
        m_new = jnp.maximum(m, m_meta[hh])
        acc = jnp.exp2(m - m_new) * acc + jnp.exp2(m_meta[hh] - m_new) * pv_meta[hh]
        outs.append(acc[:HEAD_DIM] / acc[HEAD_DIM:HEAD_DIM + 1])
    o_ref[0] = jnp.concatenate(outs, axis=0).T.astype(o_ref.dtype)


def _prompt_attention(qt, kb, vta, ckrep, kmb, vmta, cmrep):
    b, _, t = qt.shape
    assert t % TQ == 0 and TQ % TK == 0
    grid = (b, ATTN_W // LANES, t // TQ)
    return pl.pallas_call(
        _flash_kernel,
        grid=grid,
        in_specs=[
            pl.BlockSpec((1, LANES, TQ), lambda i, p, j: (i, p, j)),
            pl.BlockSpec((1, t, LANES), lambda i, p, j: (i, 0, p)),
            pl.BlockSpec((1, 2 * V_ROWS, t), lambda i, p, j: (i, p, 0)),
            pl.BlockSpec((1, 2, t, LANES), lambda i, p, j: (i, p, 0, 0)),
            pl.BlockSpec((N_META, LANES), lambda i, p, j: (0, p)),
            pl.BlockSpec((2 * V_ROWS, N_META), lambda i, p, j: (p, 0)),
            pl.BlockSpec((2, N_META, LANES), lambda i, p, j: (p, 0, 0)),
        ],
        out_specs=pl.BlockSpec((1, TQ, LANES), lambda i, p, j: (i, j, p)),
        out_shape=jax.ShapeDtypeStruct((b, t, ATTN_W), BF16),
        scratch_shapes=[pltpu.VMEM((2, 2, TK, TQ), F32)],
        compiler_params=pltpu.CompilerParams(dimension_semantics=("arbitrary",) * 3,
                                             vmem_limit_bytes=VMEM_LIMIT),
        name="prompt_attention",
    )(qt, kb, vta, ckrep, kmb, vmta, cmrep)


def _sample_conv_kernel(st_ref, glu_ref, w_ref, cb_ref, g_ref, b_ref, o_ref):
    w = w_ref[...]
    acc = w[TAIL:TAIL + 1, :] * glu_ref[...]
    for j in range(TAIL):
        acc = acc + w[j:j + 1, :] * st_ref[j]
    y = _layer_norm(acc + cb_ref[...], g_ref[...], b_ref[...])
    o_ref[...] = _silu(y).astype(o_ref.dtype)


def _sample_conv(state_t, glus, conv_w, conv_b, g, b):
    n = glus.shape[0]
    return pl.pallas_call(
        _sample_conv_kernel,
        out_shape=jax.ShapeDtypeStruct((n, CONV_CH), BF16),
        compiler_params=pltpu.CompilerParams(vmem_limit_bytes=VMEM_LIMIT),
        name="sample_conv",
    )(state_t, glus, conv_w, conv_b, g, b)


FF_CHUNK = 1024


def _mlp_kernel(x_ref, a_ref, c_ref, woa_ref, woc_ref, g1_ref, b1_ref, wup_ref, wdn_ref, g2_ref, b2_ref,
                o_ref):
    mixed = jnp.dot(a_ref[...], woa_ref[...], preferred_element_type=F32)
    mixed = mixed + jnp.dot(c_ref[...], woc_ref[...], preferred_element_type=F32)
    h = _layer_norm(ALPHA * x_ref[...] + mixed, g1_ref[...], b1_ref[...])
    hb = h.astype(BF16)
    d_ff = wup_ref.shape[1]
    m = None
    for c0 in range(0, d_ff, FF_CHUNK):
        u = jnp.dot(hb, wup_ref[:, pl.ds(c0, FF_CHUNK)], preferred_element_type=F32)
        u = jnp.maximum(u, 0.0)
        u = (u * u).astype(BF16)
        t = jnp.dot(u, wdn_ref[pl.ds(c0, FF_CHUNK), :], preferred_element_type=F32)
        m = t if m is None else m + t
    o_ref[...] = _layer_norm(ALPHA * h + m, g2_ref[...], b2_ref[...])


def _merge_mlp(x, attn, conv, woa, woc, g1, b1, wup, wdn, g2, b2, bm):
    m, d = x.shape
    assert m % bm == 0
    row = lambda w: pl.BlockSpec((bm, w), lambda i: (i, 0))
    consts = [woa, woc, g1, b1, wup, wdn, g2, b2]
    return pl.pallas_call(
        _mlp_kernel,
        grid=(m // bm,),
        in_specs=[row(d), row(ATTN_W), row(CONV_CH)] + [_const_spec(a.shape) for a in consts],
        out_specs=row(d),
        out_shape=jax.ShapeDtypeStruct((m, d), F32),
        compiler_params=pltpu.CompilerParams(dimension_semantics=("arbitrary",),
                                             vmem_limit_bytes=VMEM_LIMIT),
        name="merge_mlp",
    )(x, attn, conv, *consts)


NBUF = 32
PGROUP = 4


def _per_head(col):
    return jnp.broadcast_to(col[:, None, :], (HEADS, HEAD_DIM, 1)).reshape(ATTN_W, 1)


def _paged_attn_kernel(pt_ref, qt_ref, knt_ref, vnt_ref, lfn_ref, lft_hbm, ck_hbm, cv_hbm, o_ref,
                       buf, sem, lfbuf, lfsem, d_scr, s_all, stat_ref, qb_ref, acc_ref, ctr, first_ref):
    t = pl.program_id(0)
    n = first_ref.shape[0]
    n_pages = s_all.shape[1]
    nbuf = buf.shape[0]
    rows = n_pages * HEADS
    ISSUED, CONSUMED, I_TRIP, I_VALUES, I_OFF = range(5)

    def issue_next():
        it, in_v, off = ctr[I_TRIP], ctr[I_VALUES], ctr[I_OFF]
        slot = ctr[ISSUED] % nbuf

        @pl.when((it <= n) & (in_v == 0))
        def _():
            pltpu.make_async_copy(ck_hbm.at[pt_ref[it * n_pages + off]], buf.at[slot], sem.at[slot]).start()

        @pl.when((it <= n) & (in_v == 1))
        def _():
            pltpu.make_async_copy(cv_hbm.at[pt_ref[(it - 1) * n_pages + off]], buf.at[slot],
                                  sem.at[slot]).start()

        ctr[ISSUED] = ctr[ISSUED] + 1
        seg_done = off + 1 == n_pages
        to_values = seg_done & (in_v == 0) & (it >= 1)
        to_next = seg_done & ((in_v == 1) | (it == 0))
        nxt = it + 1
        nxt_values = nxt >= n
        ctr[I_TRIP] = jnp.where(to_next, nxt, it)
        ctr[I_VALUES] = jnp.where(to_values, 1, jnp.where(to_next, nxt_values.astype(jnp.int32), in_v))
        first_cur = first_ref[jnp.clip(it - 1, 0, n - 1)]
        first_nxt = first_ref[jnp.minimum(it, n - 1)]
        ctr[I_OFF] = jnp.where(to_values, first_cur,
                               jnp.where(to_next, jnp.where(nxt_values, first_nxt, 0), off + 1))

    def consume():
        slot = ctr[CONSUMED] % nbuf
        pltpu.make_async_copy(ck_hbm.at[0], buf.at[slot], sem.at[slot]).wait()
        ctr[CONSUMED] = ctr[CONSUMED] + 1
        return slot

    def lf_copy(seq, slot, p):
        return pltpu.make_async_copy(lft_hbm.at[pt_ref[seq * n_pages + p]], lfbuf.at[slot, p], lfsem.at[slot])

    def lf_start(seq, slot):
        def body(p, _):
            lf_copy(seq, slot, p).start()
            return 0
        lax.fori_loop(0, n_pages, body, 0)

    @pl.when(t == 0)
    def _():
        for i in range(5):
            ctr[i] = 0

        def clear(i, _):
            first_ref[i] = 0
            return 0
        lax.fori_loop(0, n, clear, 0)
        for _ in range(nbuf):
            issue_next()
        lf_start(0, 0)

    @pl.when(t + 1 < n)
    def _():
        lf_start(t + 1, (t + 1) % 2)

    seq_lane = lax.broadcasted_iota(jnp.int32, qt_ref.shape, 1)
    pick = lambda ref, seq: jnp.sum(jnp.where(seq_lane == seq, ref[...], 0.0), axis=1, keepdims=True)

    @pl.when(t < n)
    def _():
        _paged_keys(t, n_pages, rows, pick, consume, issue_next, lf_copy,
                    qt_ref, knt_ref, lfn_ref, buf, lfbuf, d_scr, s_all, stat_ref, qb_ref, first_ref)

    @pl.when(t >= 1)
    def _():
        _paged_values(t - 1, n_pages, pick, consume, issue_next, vnt_ref, buf, s_all, stat_ref, acc_ref,
                      first_ref, o_ref)


def _paged_keys(b, n_pages, rows, pick, consume, issue_next, lf_copy,
                qt_ref, knt_ref, lfn_ref, buf, lfbuf, d_scr, s_all, stat_ref, qb_ref, first_ref):
    par = b % 2

    def lf_wait(p, _):
        lf_copy(b, par, p).wait()
        return 0
    lax.fori_loop(0, n_pages, lf_wait, 0)

    lf = lfbuf[par].reshape(rows, PAGE)
    s_in = lax.broadcasted_iota(jnp.int32, (PAGE, PAGE), 0)
    s_out = lax.broadcasted_iota(jnp.int32, (PAGE, PAGE), 1)
    later = (s_in > s_out).astype(BF16)
    local = _dot_exact01(lf, later, lhs=True)
    tot = _dot_exact01(lf, jnp.ones((PAGE, PAGE), BF16), lhs=True)
    later_pages = lfn_ref[0]
    for p in reversed(range(n_pages)):
        sl = pl.ds(p * HEADS, HEADS)
        d_scr[sl, :] = (local[p * HEADS:(p + 1) * HEADS] + later_pages) * LOG2E
        later_pages = later_pages + tot[p * HEADS:(p + 1) * HEADS]

    qcol, kcol = pick(qt_ref, b), pick(knt_ref, b)
    qb_ref[...] = jnp.broadcast_to(qcol, (ATTN_W, PAGE))
    s_new = jnp.sum((qcol * kcol).reshape(HEADS, HEAD_DIM, 1), axis=1)

    def key_group(g, _):
        for i in range(PGROUP):
            u = g * PGROUP + i
            slot = consume()
            prod = buf[slot] * qb_ref[...]
            s = jnp.sum(prod.reshape(HEADS, HEAD_DIM, PAGE), axis=1)
            s_all[par, u] = s + d_scr[pl.ds(pl.multiple_of(u * HEADS, HEADS), HEADS), :]
            issue_next()
        return 0

    lax.fori_loop(0, n_pages // PGROUP, key_group, 0)

    s = s_all[par]
    m = jnp.max(jnp.max(s, axis=0), axis=1, keepdims=True)
    m = jnp.maximum(m, s_new)
    p = jnp.exp2(s - m[None])
    p_new = jnp.exp2(s_new - m)
    l = jnp.sum(jnp.sum(p, axis=0), axis=1, keepdims=True) + p_new
    s_all[par] = p
    stat_ref[par, 0] = jnp.broadcast_to(l, (HEADS, PAGE))
    stat_ref[par, 1] = jnp.broadcast_to(p_new, (HEADS, PAGE))

    page_max = jnp.max(jnp.max(p, axis=1), axis=1, keepdims=True)
    page_idx = lax.broadcasted_iota(jnp.int32, (n_pages, 1), 0)
    first = jnp.min(jnp.where(page_max > 0.0, page_idx, n_pages - 1))
    last = b == first_ref.shape[0] - 1
    first_ref[b] = jnp.where(last, 0, (first // PGROUP) * PGROUP)


def _paged_values(b, n_pages, pick, consume, issue_next, vnt_ref, buf, s_all, stat_ref, acc_ref, first_ref,
                  o_ref):
    par = b % 2
    vcol = pick(vnt_ref, b)
    l = stat_ref[par, 0][:, :1]
    p_new = stat_ref[par, 1][:, :1]
    pos = lax.broadcasted_iota(jnp.int32, (ATTN_W, PAGE), 1)
    acc_ref[...] = jnp.where(pos == 0, _per_head(p_new) * vcol, 0.0)

    def value_group(g, _):
        part = None
        for i in range(PGROUP):
            u = g * PGROUP + i
            slot = consume()
            pe = jnp.broadcast_to(s_all[par, u][:, None, :], (HEADS, HEAD_DIM, PAGE)).reshape(ATTN_W, PAGE)
            t = buf[slot] * pe
            part = t if part is None else part + t
            issue_next()
        acc_ref[...] += part
        return 0

    lax.fori_loop(first_ref[b] // PGROUP, n_pages // PGROUP, value_group, 0)
    ocol = jnp.sum(acc_ref[...], axis=1, keepdims=True) / _per_head(l)

    @pl.when(b == 0)
    def _():
        o_ref[...] = jnp.zeros_like(o_ref)

    seq_lane = lax.broadcasted_iota(jnp.int32, o_ref.shape, 1)
    o_ref[...] = jnp.where(seq_lane == b, ocol, o_ref[...])


def _paged_attention(pt_flat, qt, knt, vnt, lfn_rep, lft_pool, cache_kt, cache_vt, n_pages):
    n = qt.shape[1]
    nbuf = min(NBUF, n_pages // 2)
    assert n_pages % PGROUP == 0 and nbuf >= 1
    rows = n_pages * HEADS
    full = pl.BlockSpec((ATTN_W, n), lambda i, pt: (0, 0))
    hbm = pl.BlockSpec(memory_space=pl.ANY)
    grid_spec = pltpu.PrefetchScalarGridSpec(
        num_scalar_prefetch=1,
        grid=(n + 1,),
        in_specs=[full, full, full,
                  pl.BlockSpec((1, HEADS, PAGE), lambda i, pt: (jnp.minimum(i, n - 1), 0, 0)), hbm, hbm, hbm],
        out_specs=full,
        scratch_shapes=[pltpu.VMEM((nbuf, ATTN_W, PAGE), F32), pltpu.SemaphoreType.DMA((nbuf,)),
                        pltpu.VMEM((2, n_pages, HEADS, PAGE), F32), pltpu.SemaphoreType.DMA((2,)),
                        pltpu.VMEM((rows, PAGE), F32),
                        pltpu.VMEM((2, n_pages, HEADS, PAGE), F32),
                        pltpu.VMEM((2, 2, HEADS, PAGE), F32),
                        pltpu.VMEM((ATTN_W, PAGE), F32), pltpu.VMEM((ATTN_W, PAGE), F32),
                        pltpu.SMEM((8,), jnp.int32), pltpu.SMEM((n,), jnp.int32)],
    )
    return pl.pallas_call(
        _paged_attn_kernel,
        grid_spec=grid_spec,
        out_shape=jax.ShapeDtypeStruct((ATTN_W, n), F32),
        compiler_params=pltpu.CompilerParams(dimension_semantics=("arbitrary",),
                                             vmem_limit_bytes=VMEM_LIMIT),
        name="paged_attention",
    )(pt_flat, qt, knt, vnt, lfn_rep, lft_pool, cache_kt, cache_vt)


def _pick_bm(m, target):
    bm = min(m, target)
    while m % bm:
        bm //= 2
    return bm


def _positions_minor(x):
    nd = x.ndim
    return jnp.transpose(x, (0, 1, nd - 1) + tuple(range(2, nd - 1)))


def kernel(x_prompt, x_sample, cache_k, cache_v, cache_logf, state_conv, page_table, meta_tokens, w_in, b_f,
           conv_w, conv_b, conv_ln_g, conv_ln_b, w_out, ln1_g, ln1_b, w_up, w_down, ln2_g, ln2_b):
    bsz, seq, d = x_prompt.shape
    dbsz, dseq, _ = x_sample.shape
    depth = w_in.shape[0]
    assert depth == 1 and dseq == 1
    n_pool = cache_k.shape[1]
    n_pages = page_table.shape[1]

    w = w_in[0]
    o_f = 3 * ATTN_W
    wqkv = w[:, :o_f].astype(BF16)
    wf = jnp.pad(w[:, o_f:o_f + HEADS], ((0, 0), (0, LANES - HEADS))).astype(BF16)
    wu = w[:, o_f + HEADS:o_f + HEADS + CONV_CH].astype(BF16)
    wg = w[:, o_f + HEADS + CONV_CH:].astype(BF16)
    bf = jnp.pad(b_f[0], (0, LANES - HEADS)).reshape(1, LANES)
    woa = w_out[0, :ATTN_W].astype(BF16)
    woc = w_out[0, ATTN_W:].astype(BF16)
    wup = w_up[0].astype(BF16)
    wdn = w_down[0].astype(BF16)
    row = lambda a: a[0].reshape(1, -1)
    cw, cb, cg, cbeta = conv_w[0], row(conv_b), row(conv_ln_g), row(conv_ln_b)
    cw_rep = jnp.broadcast_to(cw[:, None, :], (CONV_W, SUBLANES, CONV_CH))
    g1, b1, g2, b2 = row(ln1_g), row(ln1_b), row(ln2_g), row(ln2_b)
    proj = functools.partial(_in_proj, wqkv=wqkv, wf=wf, wu=wu, wg=wg, bf=bf)

    meta_pad = jnp.pad(meta_tokens, ((0, LANES - N_META), (0, 0)))[None]
    _, kmb, kmt, vmt, vmta, lfm, glum = proj(meta_pad, bm=LANES)
    lfm = lfm[:N_META]
    meta_halo = jnp.concatenate([jnp.zeros((CONV_HALO - N_META, CONV_CH), F32), glum[:N_META]], axis=0)
    qt, kb, kt, vt, vta, lf, glu, conv = proj(x_prompt, bm=_pick_bm(seq, 512),
                                              conv_args=(meta_halo, cw_rep, cb, cg, cbeta))
    cmrep, ckrep, lft = _prompt_cumsum(lfm, lf.reshape(bsz, seq, LANES))
    attn = _prompt_attention(qt, kb.reshape(bsz, seq, ATTN_W), vta, ckrep, kmb[:N_META], vmta[0, :, :N_META],
                             cmrep)
    y_prompt = _merge_mlp(x_prompt.reshape(bsz * seq, d), attn.reshape(bsz * seq, ATTN_W), conv,
                          woa, woc, g1, b1, wup, wdn, g2, b2, _pick_bm(bsz * seq, 512))
    y_prompt = y_prompt.reshape(bsz, seq, d)

    def with_meta(meta_t, body_t):
        rows = body_t.shape[1]
        meta_b = jnp.broadcast_to(meta_t[None, :rows, :N_META], (bsz, rows, N_META))
        return jnp.concatenate([meta_b, body_t], axis=2)

    k_prompt = _positions_minor(with_meta(kmt[0], kt).reshape(1, bsz, HEADS, HEAD_DIM, N_META + seq))
    v_prompt = _positions_minor(with_meta(vmt[0], vt).reshape(1, bsz, HEADS, HEAD_DIM, N_META + seq))
    logf_prompt = _positions_minor(with_meta(lfm.T, lft).reshape(1, bsz, HEADS, N_META + seq))
    conv_prompt = glu.reshape(bsz, seq, CONV_CH)[:, seq - TAIL:, :].reshape(1, bsz, TAIL, CONV_CH)

    qst, kst, vst, lfs, glus = proj(x_sample.reshape(1, dbsz, d), bm=dbsz, sample=True)
    pt_flat = page_table.reshape(-1)
    cache_kt = jnp.transpose(cache_k, (0, 1, 3, 4, 2)).reshape(n_pool, ATTN_W, PAGE)
    cache_vt = jnp.transpose(cache_v, (0, 1, 3, 4, 2)).reshape(n_pool, ATTN_W, PAGE)
    lft_pool = jnp.transpose(cache_logf, (0, 1, 3, 2)).reshape(n_pool, HEADS, PAGE)
    lfn_rep = jnp.broadcast_to(lfs[:, :HEADS, None], (dbsz, HEADS, PAGE))
    attn_st = _paged_attention(pt_flat, qst[0], kst[0], vst[0], lfn_rep, lft_pool, cache_kt, cache_vt, n_pages)
    state_t = jnp.transpose(state_conv[0], (1, 0, 2))
    conv_s = _sample_conv(state_t, glus, cw, cb, cg, cbeta)
    y_sample = _merge_mlp(x_sample.reshape(dbsz, d), attn_st.T.astype(BF16), conv_s,
                          woa, woc, g1, b1, wup, wdn, g2, b2, dbsz)
    y_sample = y_sample.reshape(dbsz, 1, d)

    k_sample = jnp.transpose(kst.reshape(1, 1, HEADS, HEAD_DIM, dbsz), (0, 4, 1, 2, 3))
    v_sample = jnp.transpose(vst.reshape(1, 1, HEADS, HEAD_DIM, dbsz), (0, 4, 1, 2, 3))
    logf_sample = lfs[:, :HEADS].reshape(1, dbsz, 1, HEADS)
    conv_sample = jnp.transpose(jnp.concatenate([state_t[1:], glus[None]], axis=0), (1, 0, 2))[None]

    return (y_prompt, y_sample, k_prompt, v_prompt, logf_prompt, conv_prompt,
            k_sample, v_sample, logf_sample, conv_sample)
```

```python
import functools
import math

import jax
import jax.numpy as jnp
from jax import lax
from jax.experimental import pallas as pl
from jax.experimental.pallas import tpu as pltpu

N_META = 16
HEADS = 8
HEAD_DIM = 64
ATTN_W = HEADS * HEAD_DIM
CONV_CH = 512
CONV_W = 31
TAIL = CONV_W - 1
PAGE = 128
LANES = 128
SUBLANES = 8
BF16_ROWS = 16
ALPHA = 2.0 ** 0.25
LN_EPS = 1e-5
NEG = -1e30
LOG2E = math.log2(math.e)
QSCALE = HEAD_DIM ** -0.5 * LOG2E

F32 = jnp.float32
BF16 = jnp.bfloat16

VMEM_LIMIT = 56 * 1024 * 1024


def _const_spec(shape):
    nd = len(shape)
    return pl.BlockSpec(shape, lambda *_: (0,) * nd, pipeline_mode=pl.Buffered(1))


def _split3(x):
    hi = x.astype(BF16)
    r1 = x - hi.astype(F32)
    mid = r1.astype(BF16)
    lo = (r1 - mid.astype(F32)).astype(BF16)
    return hi, mid, lo


def _dot_exact01(x, m01, *, lhs=True):
    out = None
    for piece in _split3(x):
        if lhs:
            t = jnp.dot(piece, m01, preferred_element_type=F32)
        else:
            t = jnp.dot(m01, piece, preferred_element_type=F32)
        out = t if out is None else out + t
    return out


def _layer_norm(x, g, b):
    mu = jnp.mean(x, axis=-1, keepdims=True)
    xc = x - mu
    var = jnp.mean(xc * xc, axis=-1, keepdims=True)
    return xc * lax.rsqrt(var + LN_EPS) * g + b


def _silu(y):
    return y * (1.0 / (1.0 + jnp.exp(-y)))


V_ROWS = HEAD_DIM + BF16_ROWS
CONV_SUB = 32
CONV_HALO = 32


def _projections(x_ref, wqkv_ref, wf_ref, wu_ref, wg_ref, bf_ref):
    x = x_ref[...].reshape(x_ref.shape[-2:]).astype(BF16)
    dot = lambda w: jnp.dot(x, w, preferred_element_type=F32)
    q = dot(wqkv_ref[:, pl.ds(0, ATTN_W)]) * QSCALE
    kv = dot(wqkv_ref[:, pl.ds(ATTN_W, 2 * ATTN_W)])
    f = dot(wf_ref[...]) + bf_ref[...]
    lf = jnp.minimum(f, 0.0) - jnp.log(1.0 + jnp.exp(-jnp.abs(f)))
    glu = dot(wu_ref[...]) * (1.0 / (1.0 + jnp.exp(-dot(wg_ref[...]))))
    return q, kv[:, :ATTN_W], kv[:, ATTN_W:], lf, glu


def _store_values_with_ones(vta_ref, vt):
    ones = jnp.ones((BF16_ROWS, vt.shape[1]), BF16)
    for h in range(HEADS):
        vta_ref[0, pl.ds(h * V_ROWS, HEAD_DIM), :] = vt[h * HEAD_DIM:(h + 1) * HEAD_DIM].astype(BF16)
        vta_ref[0, pl.ds(h * V_ROWS + HEAD_DIM, BF16_ROWS), :] = ones


def _in_proj_kernel(x_ref, wqkv_ref, wf_ref, wu_ref, wg_ref, bf_ref, *out_refs, sample):
    q, k, v, lf, glu = _projections(x_ref, wqkv_ref, wf_ref, wu_ref, wg_ref, bf_ref)
    vt = v.T
    if sample:
        qt_ref, kt_ref, vt_ref, lf_ref, glu_ref = out_refs
        qt_ref[0] = q.T
    else:
        kb_ref, kt_ref, vt_ref, vta_ref, lf_ref, glu_ref = out_refs
        kb_ref[0] = k.astype(BF16)
        _store_values_with_ones(vta_ref, vt)
    kt_ref[0] = k.T
    vt_ref[0] = vt
    lf_ref[0] = lf
    glu_ref[0] = glu


def _prompt_proj_kernel(x_ref, wqkv_ref, wf_ref, wu_ref, wg_ref, bf_ref,
                        kmt_ref, vmt_ref, halo_ref, cw_ref, cb_ref, cg_ref, cbeta_ref,
                        qt_ref, kb_ref, ktf_ref, vtf_ref, vta_ref, lf_ref, glu_ref, conv_ref,
                        win_ref, sh_ref, kc_ref, vc_ref):
    j = pl.program_id(1)
    nt = pl.num_programs(1) - 1
    bm = x_ref.shape[1]

    @pl.when(j == 0)
    def _():
        kc_ref[...] = kmt_ref[...]
        vc_ref[...] = vmt_ref[...]
        win_ref[pl.ds(0, CONV_HALO), :] = halo_ref[...]

    lane = lax.broadcasted_iota(jnp.int32, (ATTN_W, LANES), 1)

    @pl.when(j < nt)
    def _():
        q, k, v, lf, glu = _projections(x_ref, wqkv_ref, wf_ref, wu_ref, wg_ref, bf_ref)
        qt_ref[0] = q.T.astype(BF16)
        kb_ref[0] = k.astype(BF16)
        vt = v.T
        _store_values_with_ones(vta_ref, vt)
        for slab, out_ref, carry_ref in ((k.T, ktf_ref, kc_ref), (vt, vtf_ref, vc_ref)):
            rolled = pltpu.roll(slab, N_META, axis=1)
            out_ref[0, :, pl.ds(0, LANES)] = jnp.where(lane < N_META, carry_ref[...], rolled[:, :LANES])
            out_ref[0, :, pl.ds(LANES, bm - LANES)] = rolled[:, LANES:]
            carry_ref[...] = rolled[:, :LANES]
        lf_ref[0] = lf
        glu_ref[0] = glu

        win_ref[pl.ds(CONV_HALO, bm), :] = glu
        span = CONV_HALO + bm - SUBLANES
        for r in range(1, SUBLANES):
            sh_ref[r, pl.ds(0, span), :] = win_ref[pl.ds(r, span), :]
        base = CONV_HALO - TAIL
        groups = CONV_SUB // SUBLANES
        for s0 in range(0, bm, CONV_SUB):
            acc = None
            for tap in range(CONV_W):
                off = s0 + base + tap
                r, a = off % SUBLANES, off - off % SUBLANES
                src = win_ref[pl.ds(a, CONV_SUB), :] if r == 0 else sh_ref[r, pl.ds(a, CONV_SUB), :]
                t = cw_ref[tap][None] * src.reshape(groups, SUBLANES, CONV_CH)
                acc = t if acc is None else acc + t
            y = _layer_norm(acc.reshape(CONV_SUB, CONV_CH) + cb_ref[...], cg_ref[...], cbeta_ref[...])
            conv_ref[0, pl.ds(s0, CONV_SUB), :] = _silu(y).astype(conv_ref.dtype)
        win_ref[pl.ds(0, CONV_HALO), :] = win_ref[pl.ds(bm, CONV_HALO), :]

    @pl.when(j == nt)
    def _():
        for out_ref, carry_ref in ((ktf_ref, kc_ref), (vtf_ref, vc_ref)):
            out_ref[0] = jnp.zeros(out_ref.shape[1:], F32)
            out_ref[0, :, pl.ds(0, LANES)] = carry_ref[...]


def _weight_specs(ws):
    return [_const_spec(w.shape) for w in ws]


def _in_proj(x, ws, sample):
    _, t, d = x.shape
    nat = lambda w, dt: jax.ShapeDtypeStruct((1, t, w), dt)
    tr = lambda r, dt: jax.ShapeDtypeStruct((1, r, t), dt)
    if sample:
        outs = [tr(ATTN_W, F32), tr(ATTN_W, F32), tr(ATTN_W, F32), nat(LANES, F32), nat(CONV_CH, F32)]
    else:
        outs = [nat(ATTN_W, BF16), tr(ATTN_W, F32), tr(ATTN_W, F32), tr(HEADS * V_ROWS, BF16),
                nat(LANES, F32), nat(CONV_CH, F32)]
    return pl.pallas_call(
        functools.partial(_in_proj_kernel, sample=sample),
        out_shape=outs,
        compiler_params=pltpu.CompilerParams(vmem_limit_bytes=VMEM_LIMIT),
        name="in_proj",
    )(x, *ws)


def _prompt_proj(x, ws, kmt, vmt, conv_args, bm):
    b, t, d = x.shape
    assert t % bm == 0 and bm > LANES
    nt = t // bm
    cur = lambda j: jnp.minimum(j, nt - 1)
    row = lambda w: pl.BlockSpec((1, bm, w), lambda i, j: (i, cur(j), 0))
    col = lambda r: pl.BlockSpec((1, r, bm), lambda i, j: (i, 0, cur(j)))
    shifted = pl.BlockSpec((1, ATTN_W, bm), lambda i, j: (i, 0, j))
    nat = lambda w, dt: jax.ShapeDtypeStruct((b, t, w), dt)
    tr = lambda r, tt, dt: jax.ShapeDtypeStruct((b, r, tt), dt)
    outs = [tr(ATTN_W, t, BF16), nat(ATTN_W, BF16), tr(ATTN_W, N_META + t, F32), tr(ATTN_W, N_META + t, F32),
            tr(HEADS * V_ROWS, t, BF16), nat(LANES, F32), nat(CONV_CH, F32), nat(CONV_CH, BF16)]
    specs = [col(ATTN_W), row(ATTN_W), shifted, shifted, col(HEADS * V_ROWS), row(LANES), row(CONV_CH),
             row(CONV_CH)]
    consts = list(ws) + [kmt, vmt] + list(conv_args)
    return pl.pallas_call(
        _prompt_proj_kernel,
        grid=(b, nt + 1),
        in_specs=[row(d)] + _weight_specs(consts),
        out_specs=specs,
        out_shape=outs,
        scratch_shapes=[pltpu.VMEM((CONV_HALO + bm, CONV_CH), F32),
                        pltpu.VMEM((SUBLANES, CONV_HALO + bm, CONV_CH), F32),
                        pltpu.VMEM((ATTN_W, LANES), F32), pltpu.VMEM((ATTN_W, LANES), F32)],
        compiler_params=pltpu.CompilerParams(dimension_semantics=("arbitrary", "arbitrary"),
                                             vmem_limit_bytes=VMEM_LIMIT),
        name="prompt_proj",
    )(x, *consts)


CS_CHUNK = 256


def _cumsum_kernel(lfm_ref, lf_ref, cm_ref, ck_ref, lft_ref):
    r16 = lax.broadcasted_iota(jnp.int32, (N_META, N_META), 0)
    c16 = lax.broadcasted_iota(jnp.int32, (N_META, N_META), 1)
    tri16 = (c16 <= r16).astype(BF16)
    cm = _dot_exact01(lfm_ref[...], tri16, lhs=False)
    for h in range(HEADS):
        cm_ref[h] = jnp.broadcast_to(cm[:, h:h + 1] * LOG2E, (N_META, LANES))
    carry = cm[N_META - 1:N_META, :]
    rr = lax.broadcasted_iota(jnp.int32, (CS_CHUNK, CS_CHUNK), 0)
    cc = lax.broadcasted_iota(jnp.int32, (CS_CHUNK, CS_CHUNK), 1)
    tri = (cc <= rr).astype(BF16)
    t = lf_ref.shape[1]
    for i in range(t // CS_CHUNK):
        sl = pl.ds(i * CS_CHUNK, CS_CHUNK)
        lf = lf_ref[0, sl, :]
        c = _dot_exact01(lf, tri, lhs=False) + carry
        for h in range(HEADS):
            ck_ref[0, h, sl, :] = jnp.broadcast_to(c[:, h:h + 1] * LOG2E, (CS_CHUNK, LANES))
        lft_ref[0, :, sl] = jnp.transpose(lf)[:HEADS]
        carry = c[CS_CHUNK - 1:CS_CHUNK, :]


def _prompt_cumsum(lf_meta, lf):
    b, t, _ = lf.shape
    return pl.pallas_call(
        _cumsum_kernel,
        grid=(b,),
        in_specs=[_const_spec((N_META, LANES)), pl.BlockSpec((1, t, LANES), lambda i: (i, 0, 0))],
        out_specs=[pl.BlockSpec((HEADS, N_META, LANES), lambda i: (0, 0, 0)),
                   pl.BlockSpec((1, HEADS, t, LANES), lambda i: (i, 0, 0, 0)),
                   pl.BlockSpec((1, HEADS, t), lambda i: (i, 0, 0))],
        out_shape=[jax.ShapeDtypeStruct((HEADS, N_META, LANES), F32),
                   jax.ShapeDtypeStruct((b, HEADS, t, LANES), F32),
                   jax.ShapeDtypeStruct((b, HEADS, t), F32)],
        compiler_params=pltpu.CompilerParams(dimension_semantics=("arbitrary",),
                                             vmem_limit_bytes=VMEM_LIMIT),
        name="prompt_cumsum",
    )(lf_meta, lf)


TQ = 512
TK = 256


def _flash_kernel(qt_ref, kb_ref, vta_ref, ck_ref, km_ref, vmta_ref, cm_ref, o_ref, u_scr):
    qi = pl.program_id(2)
    qt = qt_ref[0].astype(F32)
    feat = lax.broadcasted_iota(jnp.int32, (LANES, TQ), 0)
    reps = TQ // LANES
    qts = [jnp.where((feat // HEAD_DIM) == hh, qt, 0.0).astype(BF16) for hh in range(2)]

    def scores(j):
        off = pl.multiple_of(j * TK, TK)
        kblk = kb_ref[0, pl.ds(off, TK), :]
        return [jnp.dot(kblk, qts[hh], preferred_element_type=F32) for hh in range(2)]

    def stage1(j, s):
        off = pl.multiple_of(j * TK, TK)
        mbs = []
        for hh in range(2):
            u = s[hh] - jnp.concatenate([ck_ref[0, hh, pl.ds(off, TK), :]] * reps, axis=1)
            u_scr[j % 2, hh] = u
            mbs.append(jnp.max(u, axis=0, keepdims=True))
        return mbs

    def stage2(u, m_new, vta):
        p = jnp.exp2(u - m_new).astype(BF16)
        return jnp.dot(vta, p, preferred_element_type=F32)

    def values(j, hh):
        off = pl.multiple_of(j * TK, TK)
        return vta_ref[0, pl.ds(hh * V_ROWS, V_ROWS), pl.ds(off, TK)]

    s0 = scores(0)
    s_meta = [jnp.dot(km_ref[...], qts[hh], preferred_element_type=F32) for hh in range(2)]
    mbs = stage1(0, s0)
    u_meta = [s_meta[hh] - jnp.concatenate([cm_ref[hh]] * reps, axis=1) for hh in range(2)]
    m_meta = [jnp.max(u, axis=0, keepdims=True) for u in u_meta]

    def body(j, carry):
        mbs, state = carry[:2], carry[2:]
        s_next = scores(j + 1)
        out = []
        for hh in range(2):
            m, acc = state[2 * hh], state[2 * hh + 1]
            m_new = jnp.maximum(m, mbs[hh])
            pv = stage2(u_scr[j % 2, hh], m_new, values(j, hh))
            out += [m_new, jnp.exp2(m - m_new) * acc + pv]
        return tuple(stage1(j + 1, s_next)) + tuple(out)

    n_diag = TQ // TK
    n_full = qi * n_diag
    empty = (jnp.full((1, TQ), NEG, F32), jnp.zeros((V_ROWS, TQ), F32))
    carry = lax.fori_loop(0, n_full, body, tuple(mbs) + empty + empty)
    state = list(carry[2:])

    s_diag = [scores(n_full + d) for d in range(1, n_diag)]
    pv_meta = [stage2(u_meta[hh], m_meta[hh], vmta_ref[pl.ds(hh * V_ROWS, V_ROWS), :]) for hh in range(2)]
    key = lax.broadcasted_iota(jnp.int32, (TK, TQ), 0)
    qry = lax.broadcasted_iota(jnp.int32, (TK, TQ), 1)
    for d in range(n_diag):
        off = pl.multiple_of((n_full + d) * TK, TK)
        for hh in range(2):
            m, acc = state[2 * hh], state[2 * hh + 1]
            if d == 0:
                u = u_scr[n_full % 2, hh]
            else:
                u = s_diag[d - 1][hh] - jnp.concatenate([ck_ref[0, hh, pl.ds(off, TK), :]] * reps, axis=1)
            u = jnp.where(key + d * TK <= qry, u, NEG)
            m_new = jnp.maximum(m, jnp.max(u, axis=0, keepdims=True))
            pv = stage2(u, m_new, values(n_full + d, hh))
            state[2 * hh], state[2 * hh + 1] = m_new, jnp.exp2(m - m_new) * acc + pv
    outs = []
    for hh in range(2):
        m, acc = state[2 * hh], state[2 * hh + 1]
        m_new = jnp.maximum(m, m_meta[hh])
        acc = jnp.exp2(m - m_new) * acc + jnp.exp2(m_meta[hh] - m_new) * pv_meta[hh]
        outs.append(acc[:HEAD_DIM] / acc[HEAD_DIM:HEAD_DIM + 1])
    o_ref[0] = jnp.concatenate(outs, axis=0).T.astype(o_ref.dtype)


def _prompt_attention(qt, kb, vta, ckrep, kmb, vmta, cmrep):
    b, _, t = qt.shape
    assert t % TQ == 0 and TQ % TK == 0
    grid = (b, ATTN_W // LANES, t // TQ)
    return pl.pallas_call(
        _flash_kernel,
        grid=grid,
        in_specs=[
            pl.BlockSpec((1, LANES, TQ), lambda i, p, j: (i, p, j)),
            pl.BlockSpec((1, t, LANES), lambda i, p, j: (i, 0, p)),
            pl.BlockSpec((1, 2 * V_ROWS, t), lambda i, p, j: (i, p, 0)),
            pl.BlockSpec((1, 2, t, LANES), lambda i, p, j: (i, p, 0, 0)),
            pl.BlockSpec((N_META, LANES), lambda i, p, j: (0, p)),
            pl.BlockSpec((2 * V_ROWS, N_META), lambda i, p, j: (p, 0)),
            pl.BlockSpec((2, N_META, LANES), lambda i, p, j: (p, 0, 0)),
        ],
        out_specs=pl.BlockSpec((1, TQ, LANES), lambda i, p, j: (i, j, p)),
        out_shape=jax.ShapeDtypeStruct((b, t, ATTN_W), BF16),
        scratch_shapes=[pltpu.VMEM((2, 2, TK, TQ), F32)],
        compiler_params=pltpu.CompilerParams(dimension_semantics=("arbitrary",) * 3,
                                             vmem_limit_bytes=VMEM_LIMIT),
        name="prompt_attention",
    )(qt, kb, vta, ckrep, kmb, vmta, cmrep)


def _sample_conv_kernel(st_ref, glu_ref, w_ref, cb_ref, g_ref, b_ref, o_ref):
    w = w_ref[...]
    acc = w[TAIL:TAIL + 1, :] * glu_ref[...]
    for j in range(TAIL):
        acc = acc + w[j:j + 1, :] * st_ref[j]
    y = _layer_norm(acc + cb_ref[...], g_ref[...], b_ref[...])
    o_ref[...] = _silu(y).astype(o_ref.dtype)


def _sample_conv(state_t, glus, conv_w, conv_b, g, b):
    n = glus.shape[0]
    return pl.pallas_call(
        _sample_conv_kernel,
        out_shape=jax.ShapeDtypeStruct((n, CONV_CH), BF16),
        compiler_params=pltpu.CompilerParams(vmem_limit_bytes=VMEM_LIMIT),
        name="sample_conv",
    )(state_t, glus, conv_w, conv_b, g, b)


FF_CHUNK = 1024


def _mlp_kernel(x_ref, a_ref, c_ref, woa_ref, woc_ref, g1_ref, b1_ref, wup_ref, wdn_ref, g2_ref, b2_ref,
                o_ref):
    mixed = jnp.dot(a_ref[...], woa_ref[...], preferred_element_type=F32)
    mixed = mixed + jnp.dot(c_ref[...], woc_ref[...], preferred_element_type=F32)
    h = _layer_norm(ALPHA * x_ref[...] + mixed, g1_ref[...], b1_ref[...])
    hb = h.astype(BF16)
    d_ff = wup_ref.shape[1]
    m = None
    for c0 in range(0, d_ff, FF_CHUNK):
        u = jnp.dot(hb, wup_ref[:, pl.ds(c0, FF_CHUNK)], preferred_element_type=F32)
        u = jnp.maximum(u, 0.0)
        u = (u * u).astype(BF16)
        t = jnp.dot(u, wdn_ref[pl.ds(c0, FF_CHUNK), :], preferred_element_type=F32)
        m = t if m is None else m + t
    o_ref[...] = _layer_norm(ALPHA * h + m, g2_ref[...], b2_ref[...])


def _merge_mlp(x, attn, conv, woa, woc, g1, b1, wup, wdn, g2, b2, bm):
    m, d = x.shape
    assert m % bm == 0
    row = lambda w: pl.BlockSpec((bm, w), lambda i: (i, 0))
    consts = [woa, woc, g1, b1, wup, wdn, g2, b2]
    return pl.pallas_call(
        _mlp_kernel,
        grid=(m // bm,),
        in_specs=[row(d), row(ATTN_W), row(CONV_CH)] + [_const_spec(a.shape) for a in consts],
        out_specs=row(d),
        out_shape=jax.ShapeDtypeStruct((m, d), F32),
        compiler_params=pltpu.CompilerParams(dimension_semantics=("arbitrary",),
                                             vmem_limit_bytes=VMEM_LIMIT),
        name="merge_mlp",
    )(x, attn, conv, *consts)


NBUF = 32
PGROUP = 4


def _per_head(col):
    return jnp.broadcast_to(col[:, None, :], (HEADS, HEAD_DIM, 1)).reshape(ATTN_W, 1)


def _paged_attn_kernel(pt_ref, qt_ref, knt_ref, vnt_ref, lfn_ref, lft_hbm, ck_hbm, cv_hbm, o_ref,
                       buf, sem, lfbuf, lfsem, d_scr, s_all, stat_ref, qb_ref, acc_ref, ctr, first_ref):
    t = pl.program_id(0)
    n = first_ref.shape[0]
    n_pages = s_all.shape[1]
    nbuf = buf.shape[0]
    rows = n_pages * HEADS
    ISSUED, CONSUMED, I_TRIP, I_VALUES, I_OFF = range(5)

    def issue_next():
        it, in_v, off = ctr[I_TRIP], ctr[I_VALUES], ctr[I_OFF]
        slot = ctr[ISSUED] % nbuf

        @pl.when((it <= n) & (in_v == 0))
        def _():
            pltpu.make_async_copy(ck_hbm.at[pt_ref[it * n_pages + off]], buf.at[slot], sem.at[slot]).start()

        @pl.when((it <= n) & (in_v == 1))
        def _():
            pltpu.make_async_copy(cv_hbm.at[pt_ref[(it - 1) * n_pages + off]], buf.at[slot],
                                  sem.at[slot]).start()

        ctr[ISSUED] = ctr[ISSUED] + 1
        seg_done = off + 1 == n_pages
        to_values = seg_done & (in_v == 0) & (it >= 1)
        to_next = seg_done & ((in_v == 1) | (it == 0))
        nxt = it + 1
        nxt_values = nxt >= n
        ctr[I_TRIP] = jnp.where(to_next, nxt, it)
        ctr[I_VALUES] = jnp.where(to_values, 1, jnp.where(to_next, nxt_values.astype(jnp.int32), in_v))
        first_cur = first_ref[jnp.clip(it - 1, 0, n - 1)]
        first_nxt = first_ref[jnp.minimum(it, n - 1)]
        ctr[I_OFF] = jnp.where(to_values, first_cur,
                               jnp.where(to_next, jnp.where(nxt_values, first_nxt, 0), off + 1))

    def consume():
        slot = ctr[CONSUMED] % nbuf
        pltpu.make_async_copy(ck_hbm.at[0], buf.at[slot], sem.at[slot]).wait()
        ctr[CONSUMED] = ctr[CONSUMED] + 1
        return slot

    def lf_copy(seq, slot, p):
        return pltpu.make_async_copy(lft_hbm.at[pt_ref[seq * n_pages + p]], lfbuf.at[slot, p], lfsem.at[slot])

    def lf_start(seq, slot):
        def body(p, _):
            lf_copy(seq, slot, p).start()
            return 0
        lax.fori_loop(0, n_pages, body, 0)

    @pl.when(t == 0)
    def _():
        for i in range(5):
            ctr[i] = 0

        def clear(i, _):
            first_ref[i] = 0
            return 0
        lax.fori_loop(0, n, clear, 0)
        for _ in range(nbuf):
            issue_next()
        lf_start(0, 0)

    @pl.when(t + 1 < n)
    def _():
        lf_start(t + 1, (t + 1) % 2)

    seq_lane = lax.broadcasted_iota(jnp.int32, qt_ref.shape, 1)
    pick = lambda ref, seq: jnp.sum(jnp.where(seq_lane == seq, ref[...], 0.0), axis=1, keepdims=True)

    @pl.when(t < n)
    def _():
        _paged_keys(t, n_pages, rows, pick, consume, issue_next, lf_copy,
                    qt_ref, knt_ref, lfn_ref, buf, lfbuf, d_scr, s_all, stat_ref, qb_ref, first_ref)

    @pl.when(t >= 1)
    def _():
        _paged_values(t - 1, n_pages, pick, consume, issue_next, vnt_ref, buf, s_all, stat_ref, acc_ref,
                      first_ref, o_ref)


def _paged_keys(b, n_pages, rows, pick, consume, issue_next, lf_copy,
                qt_ref, knt_ref, lfn_ref, buf, lfbuf, d_scr, s_all, stat_ref, qb_ref, first_ref):
    par = b % 2

    def lf_wait(p, _):
        lf_copy(b, par, p).wait()
        return 0
    lax.fori_loop(0, n_pages, lf_wait, 0)

    lf = lfbuf[par].reshape(rows, PAGE)
    s_in = lax.broadcasted_iota(jnp.int32, (PAGE, PAGE), 0)
    s_out = lax.broadcasted_iota(jnp.int32, (PAGE, PAGE), 1)
    later = (s_in > s_out).astype(BF16)
    local = _dot_exact01(lf, later, lhs=True)
    tot = _dot_exact01(lf, jnp.ones((PAGE, PAGE), BF16), lhs=True)
    later_pages = lfn_ref[0]
    for p in reversed(range(n_pages)):
        sl = pl.ds(p * HEADS, HEADS)
        d_scr[sl, :] = (local[p * HEADS:(p + 1) * HEADS] + later_pages) * LOG2E
        later_pages = later_pages + tot[p * HEADS:(p + 1) * HEADS]

    qcol, kcol = pick(qt_ref, b), pick(knt_ref, b)
    qb_ref[...] = jnp.broadcast_to(qcol, (ATTN_W, PAGE))
    s_new = jnp.sum((qcol * kcol).reshape(HEADS, HEAD_DIM, 1), axis=1)

    def key_group(g, _):
        for i in range(PGROUP):
            u = g * PGROUP + i
            slot = consume()
            prod = buf[slot] * qb_ref[...]
            s = jnp.sum(prod.reshape(HEADS, HEAD_DIM, PAGE), axis=1)
            s_all[par, u] = s + d_scr[pl.ds(pl.multiple_of(u * HEADS, HEADS), HEADS), :]
            issue_next()
        return 0

    lax.fori_loop(0, n_pages // PGROUP, key_group, 0)

    s = s_all[par]
    m = jnp.max(jnp.max(s, axis=0), axis=1, keepdims=True)
    m = jnp.maximum(m, s_new)
    p = jnp.exp2(s - m[None])
    p_new = jnp.exp2(s_new - m)
    l = jnp.sum(jnp.sum(p, axis=0), axis=1, keepdims=True) + p_new
    s_all[par] = p
    stat_ref[par, 0] = jnp.broadcast_to(l, (HEADS, PAGE))
    stat_ref[par, 1] = jnp.broadcast_to(p_new, (HEADS, PAGE))

    page_max = jnp.max(jnp.max(p, axis=1), axis=1, keepdims=True)
    page_idx = lax.broadcasted_iota(jnp.int32, (n_pages, 1), 0)
    first = jnp.min(jnp.where(page_max > 0.0, page_idx, n_pages - 1))
    last = b == first_ref.shape[0] - 1
    first_ref[b] = jnp.where(last, 0, (first // PGROUP) * PGROUP)


def _paged_values(b, n_pages, pick, consume, issue_next, vnt_ref, buf, s_all, stat_ref, acc_ref, first_ref,
                  o_ref):
    par = b % 2
    vcol = pick(vnt_ref, b)
    l = stat_ref[par, 0][:, :1]
    p_new = stat_ref[par, 1][:, :1]
    pos = lax.broadcasted_iota(jnp.int32, (ATTN_W, PAGE), 1)
    acc_ref[...] = jnp.where(pos == 0, _per_head(p_new) * vcol, 0.0)

    def value_group(g, _):
        part = None
        for i in range(PGROUP):
            u = g * PGROUP + i
            slot = consume()
            pe = jnp.broadcast_to(s_all[par, u][:, None, :], (HEADS, HEAD_DIM, PAGE)).reshape(ATTN_W, PAGE)
            t = buf[slot] * pe
            part = t if part is None else part + t
            issue_next()
        acc_ref[...] += part
        return 0

    lax.fori_loop(first_ref[b] // PGROUP, n_pages // PGROUP, value_group, 0)
    ocol = jnp.sum(acc_ref[...], axis=1, keepdims=True) / _per_head(l)

    @pl.when(b == 0)
    def _():
        o_ref[...] = jnp.zeros_like(o_ref)

    seq_lane = lax.broadcasted_iota(jnp.int32, o_ref.shape, 1)
    o_ref[...] = jnp.where(seq_lane == b, ocol, o_ref[...])


def _paged_attention(pt_flat, qt, knt, vnt, lfn_rep, lft_pool, cache_kt, cache_vt, n_pages):
    n = qt.shape[1]
    nbuf = min(NBUF, n_pages // 2)
    assert n_pages % PGROUP == 0 and nbuf >= 1
    rows = n_pages * HEADS
    full = pl.BlockSpec((ATTN_W, n), lambda i, pt: (0, 0))
    hbm = pl.BlockSpec(memory_space=pl.ANY)
    grid_spec = pltpu.PrefetchScalarGridSpec(
        num_scalar_prefetch=1,
        grid=(n + 1,),
        in_specs=[full, full, full,
                  pl.BlockSpec((1, HEADS, PAGE), lambda i, pt: (jnp.minimum(i, n - 1), 0, 0)), hbm, hbm, hbm],
        out_specs=full,
        scratch_shapes=[pltpu.VMEM((nbuf, ATTN_W, PAGE), F32), pltpu.SemaphoreType.DMA((nbuf,)),
                        pltpu.VMEM((2, n_pages, HEADS, PAGE), F32), pltpu.SemaphoreType.DMA((2,)),
                        pltpu.VMEM((rows, PAGE), F32),
                        pltpu.VMEM((2, n_pages, HEADS, PAGE), F32),
                        pltpu.VMEM((2, 2, HEADS, PAGE), F32),
                        pltpu.VMEM((ATTN_W, PAGE), F32), pltpu.VMEM((ATTN_W, PAGE), F32),
                        pltpu.SMEM((8,), jnp.int32), pltpu.SMEM((n,), jnp.int32)],
    )
    return pl.pallas_call(
        _paged_attn_kernel,
        grid_spec=grid_spec,
        out_shape=jax.ShapeDtypeStruct((ATTN_W, n), F32),
        compiler_params=pltpu.CompilerParams(dimension_semantics=("arbitrary",),
                                             vmem_limit_bytes=VMEM_LIMIT),
        name="paged_attention",
    )(pt_flat, qt, knt, vnt, lfn_rep, lft_pool, cache_kt, cache_vt)


def _pick_bm(m, target):
    bm = min(m, target)
    while m % bm:
        bm //= 2
    return bm


def _positions_minor(x):
    nd = x.ndim
    return jnp.transpose(x, (0, 1, nd - 1) + tuple(range(2, nd - 1)))


def kernel(x_prompt, x_sample, cache_k, cache_v, cache_logf, state_conv, page_table, meta_tokens, w_in, b_f,
           conv_w, conv_b, conv_ln_g, conv_ln_b, w_out, ln1_g, ln1_b, w_up, w_down, ln2_g, ln2_b):
    bsz, seq, d = x_prompt.shape
    dbsz, dseq, _ = x_sample.shape
    depth = w_in.shape[0]
    assert depth == 1 and dseq == 1
    n_pool = cache_k.shape[1]
    n_pages = page_table.shape[1]

    w = w_in[0]
    o_f = 3 * ATTN_W
    wqkv = w[:, :o_f].astype(BF16)
    wf = jnp.pad(w[:, o_f:o_f + HEADS], ((0, 0), (0, LANES - HEADS))).astype(BF16)
    wu = w[:, o_f + HEADS:o_f + HEADS + CONV_CH].astype(BF16)
    wg = w[:, o_f + HEADS + CONV_CH:].astype(BF16)
    bf = jnp.pad(b_f[0], (0, LANES - HEADS)).reshape(1, LANES)
    woa = w_out[0, :ATTN_W].astype(BF16)
    woc = w_out[0, ATTN_W:].astype(BF16)
    wup = w_up[0].astype(BF16)
    wdn = w_down[0].astype(BF16)
    row = lambda a: a[0].reshape(1, -1)
    cw, cb, cg, cbeta = conv_w[0], row(conv_b), row(conv_ln_g), row(conv_ln_b)
    cw_rep = jnp.broadcast_to(cw[:, None, :], (CONV_W, SUBLANES, CONV_CH))
    g1, b1, g2, b2 = row(ln1_g), row(ln1_b), row(ln2_g), row(ln2_b)
    ws = (wqkv, wf, wu, wg, bf)

    meta_pad = jnp.pad(meta_tokens, ((0, LANES - N_META), (0, 0)))[None]
    kmb, kmt, vmt, vmta, lfm, glum = _in_proj(meta_pad, ws, sample=False)
    lfm = lfm[0, :N_META]
    meta_halo = jnp.concatenate([jnp.zeros((CONV_HALO - N_META, CONV_CH), F32), glum[0, :N_META]], axis=0)
    qt, kb, ktf, vtf, vta, lf, glu, conv = _prompt_proj(x_prompt, ws, kmt[0], vmt[0],
                                                        (meta_halo, cw_rep, cb, cg, cbeta), _pick_bm(seq, 512))
    cmrep, ckrep, lft = _prompt_cumsum(lfm, lf)
    attn = _prompt_attention(qt, kb, vta, ckrep, kmb[0, :N_META], vmta[0, :, :N_META], cmrep)
    y_prompt = _merge_mlp(x_prompt.reshape(bsz * seq, d), attn.reshape(bsz * seq, ATTN_W),
                          conv.reshape(bsz * seq, CONV_CH), woa, woc, g1, b1, wup, wdn, g2, b2,
                          _pick_bm(bsz * seq, 512))
    y_prompt = y_prompt.reshape(bsz, seq, d)

    k_prompt = _positions_minor(ktf.reshape(1, bsz, HEADS, HEAD_DIM, N_META + seq))
    v_prompt = _positions_minor(vtf.reshape(1, bsz, HEADS, HEAD_DIM, N_META + seq))
    lfm_t = jnp.broadcast_to(lfm.T[None, :HEADS], (bsz, HEADS, N_META))
    logf_prompt = _positions_minor(jnp.concatenate([lfm_t, lft], axis=2).reshape(1, bsz, HEADS, N_META + seq))
    conv_prompt = glu[:, seq - TAIL:, :].reshape(1, bsz, TAIL, CONV_CH)

    qst, kst, vst, lfs, glus = _in_proj(x_sample.reshape(1, dbsz, d), ws, sample=True)
    lfs, glus = lfs[0], glus[0]
    pt_flat = page_table.reshape(-1)
    cache_kt = jnp.transpose(cache_k, (0, 1, 3, 4, 2)).reshape(n_pool, ATTN_W, PAGE)
    cache_vt = jnp.transpose(cache_v, (0, 1, 3, 4, 2)).reshape(n_pool, ATTN_W, PAGE)
    lft_pool = jnp.transpose(cache_logf, (0, 1, 3, 2)).reshape(n_pool, HEADS, PAGE)
    lfn_rep = jnp.broadcast_to(lfs[:, :HEADS, None], (dbsz, HEADS, PAGE))
    attn_st = _paged_attention(pt_flat, qst[0], kst[0], vst[0], lfn_rep, lft_pool, cache_kt, cache_vt, n_pages)
    state_t = jnp.transpose(state_conv[0], (1, 0, 2))
    conv_s = _sample_conv(state_t, glus, cw, cb, cg, cbeta)
    y_sample = _merge_mlp(x_sample.reshape(dbsz, d), attn_st.T.astype(BF16), conv_s,
                          woa, woc, g1, b1, wup, wdn, g2, b2, dbsz)
    y_sample = y_sample.reshape(dbsz, 1, d)

    k_sample = jnp.transpose(kst.reshape(1, 1, HEADS, HEAD_DIM, dbsz), (0, 4, 1, 2, 3))
    v_sample = jnp.transpose(vst.reshape(1, 1, HEADS, HEAD_DIM, dbsz), (0, 4, 1, 2, 3))
    logf_sample = lfs[:, :HEADS].reshape(1, dbsz, 1, HEADS)
    conv_sample = jnp.transpose(jnp.concatenate([state_t[1:], glus[None]], axis=0), (1, 0, 2))[None]

    return (y_prompt, y_sample, k_prompt, v_prompt, logf_prompt, conv_prompt,
            k_sample, v_sample, logf_sample, conv_sample)
```

```python
import functools
import math

import jax
import jax.numpy as jnp
from jax import lax
from jax.experimental import pallas as pl
from jax.experimental.pallas import tpu as pltpu

N_META = 16
HEADS = 8
HEAD_DIM = 64
ATTN_W = HEADS * HEAD_DIM
CONV_CH = 512
CONV_W = 31
TAIL = CONV_W - 1
PAGE = 128
LANES = 128
SUBLANES = 8
BF16_ROWS = 16
ALPHA = 2.0 ** 0.25
LN_EPS = 1e-5
NEG = -1e30
LOG2E = math.log2(math.e)
QSCALE = HEAD_DIM ** -0.5 * LOG2E

F32 = jnp.float32
BF16 = jnp.bfloat16

VMEM_LIMIT = 56 * 1024 * 1024


def _const_spec(shape):
    nd = len(shape)
    return pl.BlockSpec(shape, lambda *_: (0,) * nd, pipeline_mode=pl.Buffered(1))


def _split3(x):
    hi = x.astype(BF16)
    r1 = x - hi.astype(F32)
    mid = r1.astype(BF16)
    lo = (r1 - mid.astype(F32)).astype(BF16)
    return hi, mid, lo


def _dot_exact01(x, m01, *, lhs=True):
    out = None
    for piece in _split3(x):
        if lhs:
            t = jnp.dot(piece, m01, preferred_element_type=F32)
        else:
            t = jnp.dot(m01, piece, preferred_element_type=F32)
        out = t if out is None else out + t
    return out


def _layer_norm(x, g, b):
    mu = jnp.mean(x, axis=-1, keepdims=True)
    xc = x - mu
    var = jnp.mean(xc * xc, axis=-1, keepdims=True)
    return xc * lax.rsqrt(var + LN_EPS) * g + b


def _silu(y):
    return y * (1.0 / (1.0 + jnp.exp(-y)))


V_ROWS = HEAD_DIM + BF16_ROWS
CONV_SUB = 32
CONV_HALO = 32


def _projections(x_ref, wqkv_ref, wf_ref, wu_ref, wg_ref, bf_ref):
    x = x_ref[...].reshape(x_ref.shape[-2:]).astype(BF16)
    dot = lambda w: jnp.dot(x, w, preferred_element_type=F32)
    q = dot(wqkv_ref[:, pl.ds(0, ATTN_W)]) * QSCALE
    kv = dot(wqkv_ref[:, pl.ds(ATTN_W, 2 * ATTN_W)])
    f = dot(wf_ref[...]) + bf_ref[...]
    lf = jnp.minimum(f, 0.0) - jnp.log(1.0 + jnp.exp(-jnp.abs(f)))
    glu = dot(wu_ref[...]) * (1.0 / (1.0 + jnp.exp(-dot(wg_ref[...]))))
    return q, kv[:, :ATTN_W], kv[:, ATTN_W:], lf, glu


def _store_values_with_ones(vta_ref, vt):
    ones = jnp.ones((BF16_ROWS, vt.shape[1]), BF16)
    for h in range(HEADS):
        vta_ref[0, pl.ds(h * V_ROWS, HEAD_DIM), :] = vt[h * HEAD_DIM:(h + 1) * HEAD_DIM].astype(BF16)
        vta_ref[0, pl.ds(h * V_ROWS + HEAD_DIM, BF16_ROWS), :] = ones


def _in_proj_kernel(x_ref, wqkv_ref, wf_ref, wu_ref, wg_ref, bf_ref, *out_refs, sample):
    q, k, v, lf, glu = _projections(x_ref, wqkv_ref, wf_ref, wu_ref, wg_ref, bf_ref)
    vt = v.T
    if sample:
        qt_ref, kt_ref, vt_ref, lf_ref, glu_ref = out_refs
        qt_ref[0] = q.T
    else:
        kb_ref, kt_ref, vt_ref, vta_ref, lf_ref, glu_ref = out_refs
        kb_ref[0] = k.astype(BF16)
        _store_values_with_ones(vta_ref, vt)
    kt_ref[0] = k.T
    vt_ref[0] = vt
    lf_ref[0] = lf
    glu_ref[0] = glu


def _prompt_proj_kernel(x_ref, wqkv_ref, wf_ref, wu_ref, wg_ref, bf_ref,
                        kmt_ref, vmt_ref, c0_ref, halo_ref, cw_ref, cb_ref, cg_ref, cbeta_ref,
                        qt_ref, kb_ref, ktf_ref, vtf_ref, vta_ref, ck_ref, lft_ref, glu_ref, conv_ref,
                        win_ref, sh_ref, kc_ref, vc_ref, cc_ref):
    j = pl.program_id(1)
    nt = pl.num_programs(1) - 1
    bm = x_ref.shape[1]

    @pl.when(j == 0)
    def _():
        kc_ref[...] = kmt_ref[...]
        vc_ref[...] = vmt_ref[...]
        cc_ref[...] = c0_ref[...]
        win_ref[pl.ds(0, CONV_HALO), :] = halo_ref[...]

    lane = lax.broadcasted_iota(jnp.int32, (ATTN_W, LANES), 1)

    @pl.when(j < nt)
    def _():
        q, k, v, lf, glu = _projections(x_ref, wqkv_ref, wf_ref, wu_ref, wg_ref, bf_ref)
        qt_ref[0] = q.T.astype(BF16)
        kb_ref[0] = k.astype(BF16)
        vt = v.T
        _store_values_with_ones(vta_ref, vt)
        for slab, out_ref, carry_ref in ((k.T, ktf_ref, kc_ref), (vt, vtf_ref, vc_ref)):
            rolled = pltpu.roll(slab, N_META, axis=1)
            out_ref[0, :, pl.ds(0, LANES)] = jnp.where(lane < N_META, carry_ref[...], rolled[:, :LANES])
            out_ref[0, :, pl.ds(LANES, bm - LANES)] = rolled[:, LANES:]
            carry_ref[...] = rolled[:, :LANES]
        carry = _cumsum_rows(lf, cc_ref[pl.ds(0, 1), :], ck_ref, lft_ref)
        cc_ref[...] = jnp.broadcast_to(carry, (SUBLANES, LANES))
        glu_ref[0] = glu

        win_ref[pl.ds(CONV_HALO, bm), :] = glu
        span = CONV_HALO + bm - SUBLANES
        for r in range(1, SUBLANES):
            sh_ref[r, pl.ds(0, span), :] = win_ref[pl.ds(r, span), :]
        base = CONV_HALO - TAIL
        groups = CONV_SUB // SUBLANES
        for s0 in range(0, bm, CONV_SUB):
            acc = None
            for tap in range(CONV_W):
                off = s0 + base + tap
                r, a = off % SUBLANES, off - off % SUBLANES
                src = win_ref[pl.ds(a, CONV_SUB), :] if r == 0 else sh_ref[r, pl.ds(a, CONV_SUB), :]
                t = cw_ref[tap][None] * src.reshape(groups, SUBLANES, CONV_CH)
                acc = t if acc is None else acc + t
            y = _layer_norm(acc.reshape(CONV_SUB, CONV_CH) + cb_ref[...], cg_ref[...], cbeta_ref[...])
            conv_ref[0, pl.ds(s0, CONV_SUB), :] = _silu(y).astype(conv_ref.dtype)
        win_ref[pl.ds(0, CONV_HALO), :] = win_ref[pl.ds(bm, CONV_HALO), :]

    @pl.when(j == nt)
    def _():
        for out_ref, carry_ref in ((ktf_ref, kc_ref), (vtf_ref, vc_ref)):
            out_ref[0] = jnp.zeros(out_ref.shape[1:], F32)
            out_ref[0, :, pl.ds(0, LANES)] = carry_ref[...]


def _weight_specs(ws):
    return [_const_spec(w.shape) for w in ws]


def _in_proj(x, ws, sample):
    _, t, d = x.shape
    nat = lambda w, dt: jax.ShapeDtypeStruct((1, t, w), dt)
    tr = lambda r, dt: jax.ShapeDtypeStruct((1, r, t), dt)
    if sample:
        outs = [tr(ATTN_W, F32), tr(ATTN_W, F32), tr(ATTN_W, F32), nat(LANES, F32), nat(CONV_CH, F32)]
    else:
        outs = [nat(ATTN_W, BF16), tr(ATTN_W, F32), tr(ATTN_W, F32), tr(HEADS * V_ROWS, BF16),
                nat(LANES, F32), nat(CONV_CH, F32)]
    return pl.pallas_call(
        functools.partial(_in_proj_kernel, sample=sample),
        out_shape=outs,
        compiler_params=pltpu.CompilerParams(vmem_limit_bytes=VMEM_LIMIT),
        name="in_proj",
    )(x, *ws)


def _prompt_proj(x, ws, kmt, vmt, c0, conv_args, bm):
    b, t, d = x.shape
    assert t % bm == 0 and bm > LANES and bm % CS_CHUNK == 0
    nt = t // bm
    cur = lambda j: jnp.minimum(j, nt - 1)
    row = lambda w: pl.BlockSpec((1, bm, w), lambda i, j: (i, cur(j), 0))
    col = lambda r: pl.BlockSpec((1, r, bm), lambda i, j: (i, 0, cur(j)))
    shifted = pl.BlockSpec((1, ATTN_W, bm), lambda i, j: (i, 0, j))
    nat = lambda w, dt: jax.ShapeDtypeStruct((b, t, w), dt)
    tr = lambda r, tt, dt: jax.ShapeDtypeStruct((b, r, tt), dt)
    outs = [tr(ATTN_W, t, BF16), nat(ATTN_W, BF16), tr(ATTN_W, N_META + t, F32), tr(ATTN_W, N_META + t, F32),
            tr(HEADS * V_ROWS, t, BF16), jax.ShapeDtypeStruct((b, HEADS, t, LANES), F32), tr(HEADS, t, F32),
            nat(CONV_CH, F32), nat(CONV_CH, BF16)]
    specs = [col(ATTN_W), row(ATTN_W), shifted, shifted, col(HEADS * V_ROWS),
             pl.BlockSpec((1, HEADS, bm, LANES), lambda i, j: (i, 0, cur(j), 0)), col(HEADS), row(CONV_CH),
             row(CONV_CH)]
    consts = list(ws) + [kmt, vmt, c0] + list(conv_args)
    return pl.pallas_call(
        _prompt_proj_kernel,
        grid=(b, nt + 1),
        in_specs=[row(d)] + _weight_specs(consts),
        out_specs=specs,
        out_shape=outs,
        scratch_shapes=[pltpu.VMEM((CONV_HALO + bm, CONV_CH), F32),
                        pltpu.VMEM((SUBLANES, CONV_HALO + bm, CONV_CH), F32),
                        pltpu.VMEM((ATTN_W, LANES), F32), pltpu.VMEM((ATTN_W, LANES), F32),
                        pltpu.VMEM((SUBLANES, LANES), F32)],
        compiler_params=pltpu.CompilerParams(dimension_semantics=("arbitrary", "arbitrary"),
                                             vmem_limit_bytes=VMEM_LIMIT),
        name="prompt_proj",
    )(x, *consts)


CS_CHUNK = 256


def _cumsum_rows(lf, carry, ck_ref, lft_ref):
    rr = lax.broadcasted_iota(jnp.int32, (CS_CHUNK, CS_CHUNK), 0)
    cc = lax.broadcasted_iota(jnp.int32, (CS_CHUNK, CS_CHUNK), 1)
    tri = (cc <= rr).astype(BF16)
    for i in range(lf.shape[0] // CS_CHUNK):
        sl = pl.ds(i * CS_CHUNK, CS_CHUNK)
        chunk = lf[i * CS_CHUNK:(i + 1) * CS_CHUNK]
        c = _dot_exact01(chunk, tri, lhs=False) + carry
        for h in range(HEADS):
            ck_ref[0, h, sl, :] = jnp.broadcast_to(c[:, h:h + 1] * LOG2E, (CS_CHUNK, LANES))
        lft_ref[0, :, sl] = jnp.transpose(chunk)[:HEADS]
        carry = c[CS_CHUNK - 1:CS_CHUNK, :]
    return carry


def _meta_cumsum_kernel(lfm_ref, cm_ref, carry_ref):
    r16 = lax.broadcasted_iota(jnp.int32, (N_META, N_META), 0)
    c16 = lax.broadcasted_iota(jnp.int32, (N_META, N_META), 1)
    tri16 = (c16 <= r16).astype(BF16)
    cm = _dot_exact01(lfm_ref[...], tri16, lhs=False)
    for h in range(HEADS):
        cm_ref[h] = jnp.broadcast_to(cm[:, h:h + 1] * LOG2E, (N_META, LANES))
    carry_ref[...] = jnp.broadcast_to(cm[N_META - 1:N_META, :], (SUBLANES, LANES))


def _meta_cumsum(lf_meta):
    return pl.pallas_call(
        _meta_cumsum_kernel,
        out_shape=[jax.ShapeDtypeStruct((HEADS, N_META, LANES), F32),
                   jax.ShapeDtypeStruct((SUBLANES, LANES), F32)],
        compiler_params=pltpu.CompilerParams(vmem_limit_bytes=VMEM_LIMIT),
        name="meta_cumsum",
    )(lf_meta)


TQ = 512
TK = 256


def _flash_kernel(qt_ref, kb_ref, vta_ref, ck_ref, km_ref, vmta_ref, cm_ref, o_ref, u_scr):
    qi = pl.program_id(2)
    qt = qt_ref[0].astype(F32)
    feat = lax.broadcasted_iota(jnp.int32, (LANES, TQ), 0)
    reps = TQ // LANES
    qts = [jnp.where((feat // HEAD_DIM) == hh, qt, 0.0).astype(BF16) for hh in range(2)]

    def scores(j):
        off = pl.multiple_of(j * TK, TK)
        kblk = kb_ref[0, pl.ds(off, TK), :]
        return [jnp.dot(kblk, qts[hh], preferred_element_type=F32) for hh in range(2)]

    def stage1(j, s):
        off = pl.multiple_of(j * TK, TK)
        mbs = []
        for hh in range(2):
            u = s[hh] - jnp.concatenate([ck_ref[0, hh, pl.ds(off, TK), :]] * reps, axis=1)
            u_scr[j % 2, hh] = u
            mbs.append(jnp.max(u, axis=0, keepdims=True))
        return mbs

    def stage2(u, m_new, vta):
        p = jnp.exp2(u - m_new).astype(BF16)
        return jnp.dot(vta, p, preferred_element_type=F32)

    def values(j, hh):
        off = pl.multiple_of(j * TK, TK)
        return vta_ref[0, pl.ds(hh * V_ROWS, V_ROWS), pl.ds(off, TK)]

    s0 = scores(0)
    s_meta = [jnp.dot(km_ref[...], qts[hh], preferred_element_type=F32) for hh in range(2)]
    mbs = stage1(0, s0)
    u_meta = [s_meta[hh] - jnp.concatenate([cm_ref[hh]] * reps, axis=1) for hh in range(2)]
    m_meta = [jnp.max(u, axis=0, keepdims=True) for u in u_meta]

    def body(j, carry):
        mbs, state = carry[:2], carry[2:]
        s_next = scores(j + 1)
        out = []
        for hh in range(2):
            m, acc = state[2 * hh], state[2 * hh + 1]
            m_new = jnp.maximum(m, mbs[hh])
            pv = stage2(u_scr[j % 2, hh], m_new, values(j, hh))
            out += [m_new, jnp.exp2(m - m_new) * acc + pv]
        return tuple(stage1(j + 1, s_next)) + tuple(out)

    n_diag = TQ // TK
    n_full = qi * n_diag
    empty = (jnp.full((1, TQ), NEG, F32), jnp.zeros((V_ROWS, TQ), F32))
    carry = lax.fori_loop(0, n_full, body, tuple(mbs) + empty + empty)
    state = list(carry[2:])

    s_diag = [scores(n_full + d) for d in range(1, n_diag)]
    pv_meta = [stage2(u_meta[hh], m_meta[hh], vmta_ref[pl.ds(hh * V_ROWS, V_ROWS), :]) for hh in range(2)]
    key = lax.broadcasted_iota(jnp.int32, (TK, TQ), 0)
    qry = lax.broadcasted_iota(jnp.int32, (TK, TQ), 1)
    for d in range(n_diag):
        off = pl.multiple_of((n_full + d) * TK, TK)
        for hh in range(2):
            m, acc = state[2 * hh], state[2 * hh + 1]
            if d == 0:
                u = u_scr[n_full % 2, hh]
            else:
                u = s_diag[d - 1][hh] - jnp.concatenate([ck_ref[0, hh, pl.ds(off, TK), :]] * reps, axis=1)
            u = jnp.where(key + d * TK <= qry, u, NEG)
            m_new = jnp.maximum(m, jnp.max(u, axis=0, keepdims=True))
            pv = stage2(u, m_new, values(n_full + d, hh))
            state[2 * hh], state[2 * hh + 1] = m_new, jnp.exp2(m - m_new) * acc + pv
    outs = []
    for hh in range(2):
        m, acc = state[2 * hh], state[2 * hh + 1]
        m_new = jnp.maximum(m, m_meta[hh])
        acc = jnp.exp2(m - m_new) * acc + jnp.exp2(m_meta[hh] - m_new) * pv_meta[hh]
        outs.append(acc[:HEAD_DIM] / acc[HEAD_DIM:HEAD_DIM + 1])
    o_ref[0] = jnp.concatenate(outs, axis=0).T.astype(o_ref.dtype)


def _prompt_attention(qt, kb, vta, ckrep, kmb, vmta, cmrep):
    b, _, t = qt.shape
    assert t % TQ == 0 and TQ % TK == 0
    grid = (b, ATTN_W // LANES, t // TQ)
    return pl.pallas_call(
        _flash_kernel,
        grid=grid,
        in_specs=[
            pl.BlockSpec((1, LANES, TQ), lambda i, p, j: (i, p, j)),
            pl.BlockSpec((1, t, LANES), lambda i, p, j: (i, 0, p)),
            pl.BlockSpec((1, 2 * V_ROWS, t), lambda i, p, j: (i, p, 0)),
            pl.BlockSpec((1, 2, t, LANES), lambda i, p, j: (i, p, 0, 0)),
            pl.BlockSpec((N_META, LANES), lambda i, p, j: (0, p)),
            pl.BlockSpec((2 * V_ROWS, N_META), lambda i, p, j: (p, 0)),
            pl.BlockSpec((2, N_META, LANES), lambda i, p, j: (p, 0, 0)),
        ],
        out_specs=pl.BlockSpec((1, TQ, LANES), lambda i, p, j: (i, j, p)),
        out_shape=jax.ShapeDtypeStruct((b, t, ATTN_W), BF16),
        scratch_shapes=[pltpu.VMEM((2, 2, TK, TQ), F32)],
        compiler_params=pltpu.CompilerParams(dimension_semantics=("arbitrary",) * 3,
                                             vmem_limit_bytes=VMEM_LIMIT),
        name="prompt_attention",
    )(qt, kb, vta, ckrep, kmb, vmta, cmrep)


def _sample_conv_kernel(st_ref, glu_ref, w_ref, cb_ref, g_ref, b_ref, o_ref):
    w = w_ref[...]
    acc = w[TAIL:TAIL + 1, :] * glu_ref[...]
    for j in range(TAIL):
        acc = acc + w[j:j + 1, :] * st_ref[j]
    y = _layer_norm(acc + cb_ref[...], g_ref[...], b_ref[...])
    o_ref[...] = _silu(y).astype(o_ref.dtype)


def _sample_conv(state_t, glus, conv_w, conv_b, g, b):
    n = glus.shape[0]
    return pl.pallas_call(
        _sample_conv_kernel,
        out_shape=jax.ShapeDtypeStruct((n, CONV_CH), BF16),
        compiler_params=pltpu.CompilerParams(vmem_limit_bytes=VMEM_LIMIT),
        name="sample_conv",
    )(state_t, glus, conv_w, conv_b, g, b)


FF_CHUNK = 1024


def _mlp_kernel(x_ref, a_ref, c_ref, woa_ref, woc_ref, g1_ref, b1_ref, wup_ref, wdn_ref, g2_ref, b2_ref,
                o_ref):
    mixed = jnp.dot(a_ref[...], woa_ref[...], preferred_element_type=F32)
    mixed = mixed + jnp.dot(c_ref[...], woc_ref[...], preferred_element_type=F32)
    h = _layer_norm(ALPHA * x_ref[...] + mixed, g1_ref[...], b1_ref[...])
    hb = h.astype(BF16)
    d_ff = wup_ref.shape[1]
    m = None
    for c0 in range(0, d_ff, FF_CHUNK):
        u = jnp.dot(hb, wup_ref[:, pl.ds(c0, FF_CHUNK)], preferred_element_type=F32)
        u = jnp.maximum(u, 0.0)
        u = (u * u).astype(BF16)
        t = jnp.dot(u, wdn_ref[pl.ds(c0, FF_CHUNK), :], preferred_element_type=F32)
        m = t if m is None else m + t
    o_ref[...] = _layer_norm(ALPHA * h + m, g2_ref[...], b2_ref[...])


def _merge_mlp(x, attn, conv, woa, woc, g1, b1, wup, wdn, g2, b2, bm):
    m, d = x.shape
    assert m % bm == 0
    row = lambda w: pl.BlockSpec((bm, w), lambda i: (i, 0))
    consts = [woa, woc, g1, b1, wup, wdn, g2, b2]
    return pl.pallas_call(
        _mlp_kernel,
        grid=(m // bm,),
        in_specs=[row(d), row(ATTN_W), row(CONV_CH)] + [_const_spec(a.shape) for a in consts],
        out_specs=row(d),
        out_shape=jax.ShapeDtypeStruct((m, d), F32),
        compiler_params=pltpu.CompilerParams(dimension_semantics=("arbitrary",),
                                             vmem_limit_bytes=VMEM_LIMIT),
        name="merge_mlp",
    )(x, attn, conv, *consts)


NBUF = 32
PGROUP = 8


def _per_head(col):
    return jnp.broadcast_to(col[:, None, :], (HEADS, HEAD_DIM, 1)).reshape(ATTN_W, 1)


def _paged_attn_kernel(pt_ref, qt_ref, knt_ref, vnt_ref, lfn_ref, lft_hbm, ck_hbm, cv_hbm, o_ref,
                       buf, sem, lfbuf, lfsem, d_scr, s_all, stat_ref, qb_ref, acc_ref, ctr, first_ref):
    t = pl.program_id(0)
    n = first_ref.shape[0]
    n_pages = s_all.shape[1]
    nbuf = buf.shape[0]
    rows = n_pages * HEADS
    ISSUED, CONSUMED, I_TRIP, I_VALUES, I_OFF = range(5)

    def issue_next():
        it, in_v, off = ctr[I_TRIP], ctr[I_VALUES], ctr[I_OFF]
        slot = ctr[ISSUED] % nbuf

        @pl.when((it <= n) & (in_v == 0))
        def _():
            pltpu.make_async_copy(ck_hbm.at[pt_ref[it * n_pages + off]], buf.at[slot], sem.at[slot]).start()

        @pl.when((it <= n) & (in_v == 1))
        def _():
            pltpu.make_async_copy(cv_hbm.at[pt_ref[(it - 1) * n_pages + off]], buf.at[slot],
                                  sem.at[slot]).start()

        ctr[ISSUED] = ctr[ISSUED] + 1
        seg_done = off + 1 == n_pages
        to_values = seg_done & (in_v == 0) & (it >= 1)
        to_next = seg_done & ((in_v == 1) | (it == 0))
        nxt = it + 1
        nxt_values = nxt >= n
        ctr[I_TRIP] = jnp.where(to_next, nxt, it)
        ctr[I_VALUES] = jnp.where(to_values, 1, jnp.where(to_next, nxt_values.astype(jnp.int32), in_v))
        first_cur = first_ref[jnp.clip(it - 1, 0, n - 1)]
        first_nxt = first_ref[jnp.minimum(it, n - 1)]
        ctr[I_OFF] = jnp.where(to_values, first_cur,
                               jnp.where(to_next, jnp.where(nxt_values, first_nxt, 0), off + 1))

    def consume():
        slot = ctr[CONSUMED] % nbuf
        pltpu.make_async_copy(ck_hbm.at[0], buf.at[slot], sem.at[slot]).wait()
        ctr[CONSUMED] = ctr[CONSUMED] + 1
        return slot

    def lf_copy(seq, slot, p):
        return pltpu.make_async_copy(lft_hbm.at[pt_ref[seq * n_pages + p]], lfbuf.at[slot, p], lfsem.at[slot])

    def lf_start(seq, slot):
        def body(p, _):
            lf_copy(seq, slot, p).start()
            return 0
        lax.fori_loop(0, n_pages, body, 0)

    @pl.when(t == 0)
    def _():
        for i in range(5):
            ctr[i] = 0

        def clear(i, _):
            first_ref[i] = 0
            return 0
        lax.fori_loop(0, n, clear, 0)
        for _ in range(nbuf):
            issue_next()
        lf_start(0, 0)

    @pl.when(t + 1 < n)
    def _():
        lf_start(t + 1, (t + 1) % 2)

    seq_lane = lax.broadcasted_iota(jnp.int32, qt_ref.shape, 1)
    pick = lambda ref, seq: jnp.sum(jnp.where(seq_lane == seq, ref[...], 0.0), axis=1, keepdims=True)

    @pl.when(t < n)
    def _():
        _paged_keys(t, n_pages, rows, pick, consume, issue_next, lf_copy,
                    qt_ref, knt_ref, lfn_ref, buf, lfbuf, d_scr, s_all, stat_ref, qb_ref, first_ref)

    @pl.when(t >= 1)
    def _():
        _paged_values(t - 1, n_pages, pick, consume, issue_next, vnt_ref, buf, s_all, stat_ref, acc_ref,
                      first_ref, o_ref)


def _paged_keys(b, n_pages, rows, pick, consume, issue_next, lf_copy,
                qt_ref, knt_ref, lfn_ref, buf, lfbuf, d_scr, s_all, stat_ref, qb_ref, first_ref):
    par = b % 2

    def lf_wait(p, _):
        lf_copy(b, par, p).wait()
        return 0
    lax.fori_loop(0, n_pages, lf_wait, 0)

    lf = lfbuf[par].reshape(rows, PAGE)
    s_in = lax.broadcasted_iota(jnp.int32, (PAGE, PAGE), 0)
    s_out = lax.broadcasted_iota(jnp.int32, (PAGE, PAGE), 1)
    later = (s_in > s_out).astype(BF16)
    local = _dot_exact01(lf, later, lhs=True)
    tot = _dot_exact01(lf, jnp.ones((PAGE, PAGE), BF16), lhs=True)
    later_pages = lfn_ref[0]
    for p in reversed(range(n_pages)):
        sl = pl.ds(p * HEADS, HEADS)
        d_scr[sl, :] = (local[p * HEADS:(p + 1) * HEADS] + later_pages) * LOG2E
        later_pages = later_pages + tot[p * HEADS:(p + 1) * HEADS]

    qcol, kcol = pick(qt_ref, b), pick(knt_ref, b)
    qb_ref[...] = jnp.broadcast_to(qcol, (ATTN_W, PAGE))
    s_new = jnp.sum((qcol * kcol).reshape(HEADS, HEAD_DIM, 1), axis=1)

    def key_group(g, _):
        for i in range(PGROUP):
            u = g * PGROUP + i
            slot = consume()
            prod = buf[slot] * qb_ref[...]
            s = jnp.sum(prod.reshape(HEADS, HEAD_DIM, PAGE), axis=1)
            s_all[par, u] = s + d_scr[pl.ds(pl.multiple_of(u * HEADS, HEADS), HEADS), :]
            issue_next()
        return 0

    lax.fori_loop(0, n_pages // PGROUP, key_group, 0)

    s = s_all[par]
    m = jnp.max(jnp.max(s, axis=0), axis=1, keepdims=True)
    m = jnp.maximum(m, s_new)
    p = jnp.exp2(s - m[None])
    p_new = jnp.exp2(s_new - m)
    l = jnp.sum(jnp.sum(p, axis=0), axis=1, keepdims=True) + p_new
    s_all[par] = p
    stat_ref[par, 0] = jnp.broadcast_to(l, (HEADS, PAGE))
    stat_ref[par, 1] = jnp.broadcast_to(p_new, (HEADS, PAGE))

    page_max = jnp.max(jnp.max(p, axis=1), axis=1, keepdims=True)
    page_idx = lax.broadcasted_iota(jnp.int32, (n_pages, 1), 0)
    first = jnp.min(jnp.where(page_max > 0.0, page_idx, n_pages - 1))
    last = b == first_ref.shape[0] - 1
    first_ref[b] = jnp.where(last, 0, (first // PGROUP) * PGROUP)


def _paged_values(b, n_pages, pick, consume, issue_next, vnt_ref, buf, s_all, stat_ref, acc_ref, first_ref,
                  o_ref):
    par = b % 2
    vcol = pick(vnt_ref, b)
    l = stat_ref[par, 0][:, :1]
    p_new = stat_ref[par, 1][:, :1]
    pos = lax.broadcasted_iota(jnp.int32, (ATTN_W, PAGE), 1)
    acc_ref[...] = jnp.where(pos == 0, _per_head(p_new) * vcol, 0.0)

    def value_group(g, _):
        part = None
        for i in range(PGROUP):
            u = g * PGROUP + i
            slot = consume()
            pe = jnp.broadcast_to(s_all[par, u][:, None, :], (HEADS, HEAD_DIM, PAGE)).reshape(ATTN_W, PAGE)
            t = buf[slot] * pe
            part = t if part is None else part + t
            issue_next()
        acc_ref[...] += part
        return 0

    lax.fori_loop(first_ref[b] // PGROUP, n_pages // PGROUP, value_group, 0)
    ocol = jnp.sum(acc_ref[...], axis=1, keepdims=True) / _per_head(l)

    @pl.when(b == 0)
    def _():
        o_ref[...] = jnp.zeros_like(o_ref)

    seq_lane = lax.broadcasted_iota(jnp.int32, o_ref.shape, 1)
    o_ref[...] = jnp.where(seq_lane == b, ocol, o_ref[...])


def _paged_attention(pt_flat, qt, knt, vnt, lfn_rep, lft_pool, cache_kt, cache_vt, n_pages):
    n = qt.shape[1]
    nbuf = min(NBUF, n_pages // 2)
    assert n_pages % PGROUP == 0 and nbuf >= 1
    rows = n_pages * HEADS
    full = pl.BlockSpec((ATTN_W, n), lambda i, pt: (0, 0))
    hbm = pl.BlockSpec(memory_space=pl.ANY)
    grid_spec = pltpu.PrefetchScalarGridSpec(
        num_scalar_prefetch=1,
        grid=(n + 1,),
        in_specs=[full, full, full,
                  pl.BlockSpec((1, HEADS, PAGE), lambda i, pt: (jnp.minimum(i, n - 1), 0, 0)), hbm, hbm, hbm],
        out_specs=full,
        scratch_shapes=[pltpu.VMEM((nbuf, ATTN_W, PAGE), F32), pltpu.SemaphoreType.DMA((nbuf,)),
                        pltpu.VMEM((2, n_pages, HEADS, PAGE), F32), pltpu.SemaphoreType.DMA((2,)),
                        pltpu.VMEM((rows, PAGE), F32),
                        pltpu.VMEM((2, n_pages, HEADS, PAGE), F32),
                        pltpu.VMEM((2, 2, HEADS, PAGE), F32),
                        pltpu.VMEM((ATTN_W, PAGE), F32), pltpu.VMEM((ATTN_W, PAGE), F32),
                        pltpu.SMEM((8,), jnp.int32), pltpu.SMEM((n,), jnp.int32)],
    )
    return pl.pallas_call(
        _paged_attn_kernel,
        grid_spec=grid_spec,
        out_shape=jax.ShapeDtypeStruct((ATTN_W, n), F32),
        compiler_params=pltpu.CompilerParams(dimension_semantics=("arbitrary",),
                                             vmem_limit_bytes=VMEM_LIMIT),
        name="paged_attention",
    )(pt_flat, qt, knt, vnt, lfn_rep, lft_pool, cache_kt, cache_vt)


def _pick_bm(m, target):
    bm = min(m, target)
    while m % bm:
        bm //= 2
    return bm


def _positions_minor(x):
    nd = x.ndim
    return jnp.transpose(x, (0, 1, nd - 1) + tuple(range(2, nd - 1)))


def kernel(x_prompt, x_sample, cache_k, cache_v, cache_logf, state_conv, page_table, meta_tokens, w_in, b_f,
           conv_w, conv_b, conv_ln_g, conv_ln_b, w_out, ln1_g, ln1_b, w_up, w_down, ln2_g, ln2_b):
    bsz, seq, d = x_prompt.shape
    dbsz, dseq, _ = x_sample.shape
    depth = w_in.shape[0]
    assert depth == 1 and dseq == 1
    n_pool = cache_k.shape[1]
    n_pages = page_table.shape[1]

    w = w_in[0]
    o_f = 3 * ATTN_W
    wqkv = w[:, :o_f].astype(BF16)
    wf = jnp.pad(w[:, o_f:o_f + HEADS], ((0, 0), (0, LANES - HEADS))).astype(BF16)
    wu = w[:, o_f + HEADS:o_f + HEADS + CONV_CH].astype(BF16)
    wg = w[:, o_f + HEADS + CONV_CH:].astype(BF16)
    bf = jnp.pad(b_f[0], (0, LANES - HEADS)).reshape(1, LANES)
    woa = w_out[0, :ATTN_W].astype(BF16)
    woc = w_out[0, ATTN_W:].astype(BF16)
    wup = w_up[0].astype(BF16)
    wdn = w_down[0].astype(BF16)
    row = lambda a: a[0].reshape(1, -1)
    cw, cb, cg, cbeta = conv_w[0], row(conv_b), row(conv_ln_g), row(conv_ln_b)
    cw_rep = jnp.broadcast_to(cw[:, None, :], (CONV_W, SUBLANES, CONV_CH))
    g1, b1, g2, b2 = row(ln1_g), row(ln1_b), row(ln2_g), row(ln2_b)
    ws = (wqkv, wf, wu, wg, bf)

    meta_pad = jnp.pad(meta_tokens, ((0, LANES - N_META), (0, 0)))[None]
    kmb, kmt, vmt, vmta, lfm, glum = _in_proj(meta_pad, ws, sample=False)
    lfm = lfm[0, :N_META]
    meta_halo = jnp.concatenate([jnp.zeros((CONV_HALO - N_META, CONV_CH), F32), glum[0, :N_META]], axis=0)
    cmrep, c0 = _meta_cumsum(lfm)
    qt, kb, ktf, vtf, vta, ckrep, lft, glu, conv = _prompt_proj(
        x_prompt, ws, kmt[0], vmt[0], c0, (meta_halo, cw_rep, cb, cg, cbeta), _pick_bm(seq, 512))
    attn = _prompt_attention(qt, kb, vta, ckrep, kmb[0, :N_META], vmta[0, :, :N_META], cmrep)
    y_prompt = _merge_mlp(x_prompt.reshape(bsz * seq, d), attn.reshape(bsz * seq, ATTN_W),
                          conv.reshape(bsz * seq, CONV_CH), woa, woc, g1, b1, wup, wdn, g2, b2,
                          _pick_bm(bsz * seq, 512))
    y_prompt = y_prompt.reshape(bsz, seq, d)

    k_prompt = _positions_minor(ktf.reshape(1, bsz, HEADS, HEAD_DIM, N_META + seq))
    v_prompt = _positions_minor(vtf.reshape(1, bsz, HEADS, HEAD_DIM, N_META + seq))
    lfm_t = jnp.broadcast_to(lfm.T[None, :HEADS], (bsz, HEADS, N_META))
    logf_prompt = _positions_minor(jnp.concatenate([lfm_t, lft], axis=2).reshape(1, bsz, HEADS, N_META + seq))
    conv_prompt = glu[:, seq - TAIL:, :].reshape(1, bsz, TAIL, CONV_CH)

    qst, kst, vst, lfs, glus = _in_proj(x_sample.reshape(1, dbsz, d), ws, sample=True)
    lfs, glus = lfs[0], glus[0]
    pt_flat = page_table.reshape(-1)
    cache_kt = jnp.transpose(cache_k, (0, 1, 3, 4, 2)).reshape(n_pool, ATTN_W, PAGE)
    cache_vt = jnp.transpose(cache_v, (0, 1, 3, 4, 2)).reshape(n_pool, ATTN_W, PAGE)
    lft_pool = jnp.transpose(cache_logf, (0, 1, 3, 2)).reshape(n_pool, HEADS, PAGE)
    lfn_rep = jnp.broadcast_to(lfs[:, :HEADS, None], (dbsz, HEADS, PAGE))
    attn_st = _paged_attention(pt_flat, qst[0], kst[0], vst[0], lfn_rep, lft_pool, cache_kt, cache_vt, n_pages)
    state_t = jnp.transpose(state_conv[0], (1, 0, 2))
    conv_s = _sample_conv(state_t, glus, cw, cb, cg, cbeta)
    y_sample = _merge_mlp(x_sample.reshape(dbsz, d), attn_st.T.astype(BF16), conv_s,
                          woa, woc, g1, b1, wup, wdn, g2, b2, dbsz)
    y_sample = y_sample.reshape(dbsz, 1, d)

    k_sample = jnp.transpose(kst.reshape(1, 1, HEADS, HEAD_DIM, dbsz), (0, 4, 1, 2, 3))
    v_sample = jnp.transpose(vst.reshape(1, 1, HEADS, HEAD_DIM, dbsz), (0, 4, 1, 2, 3))
    logf_sample = lfs[:, :HEADS].reshape(1, dbsz, 1, HEADS)
    conv_sample = jnp.transpose(jnp.concatenate([state_t[1:], glus[None]], axis=0), (1, 0, 2))[None]

    return (y_prompt, y_sample, k_prompt, v_prompt, logf_prompt, conv_prompt,
            k_sample, v_sample, logf_sample, conv_sample)
```

```python
import functools
import math

import jax
import jax.numpy as jnp
from jax import lax
from jax.experimental import pallas as pl
from jax.experimental.pallas import tpu as pltpu

N_META = 16
HEADS = 8
HEAD_DIM = 64
ATTN_W = HEADS * HEAD_DIM
CONV_CH = 512
CONV_W = 31
TAIL = CONV_W - 1
PAGE = 128
LANES = 128
SUBLANES = 8
BF16_ROWS = 16
ALPHA = 2.0 ** 0.25
LN_EPS = 1e-5
NEG = -1e30
LOG2E = math.log2(math.e)
QSCALE = HEAD_DIM ** -0.5 * LOG2E

F32 = jnp.float32
BF16 = jnp.bfloat16

VMEM_LIMIT = 56 * 1024 * 1024


def _const_spec(shape):
    nd = len(shape)
    return pl.BlockSpec(shape, lambda *_: (0,) * nd, pipeline_mode=pl.Buffered(1))


def _split3(x):
    hi = x.astype(BF16)
    r1 = x - hi.astype(F32)
    mid = r1.astype(BF16)
    lo = (r1 - mid.astype(F32)).astype(BF16)
    return hi, mid, lo


def _dot_exact01(x, m01, *, lhs=True):
    out = None
    for piece in _split3(x):
        if lhs:
            t = jnp.dot(piece, m01, preferred_element_type=F32)
        else:
            t = jnp.dot(m01, piece, preferred_element_type=F32)
        out = t if out is None else out + t
    return out


def _layer_norm(x, g, b):
    mu = jnp.mean(x, axis=-1, keepdims=True)
    xc = x - mu
    var = jnp.mean(xc * xc, axis=-1, keepdims=True)
    return xc * lax.rsqrt(var + LN_EPS) * g + b


def _silu(y):
    return y * (1.0 / (1.0 + jnp.exp(-y)))


V_ROWS = HEAD_DIM + BF16_ROWS
CONV_SUB = 32
CONV_HALO = 32


def _projections(x_ref, wqkv_ref, wf_ref, wu_ref, wg_ref, bf_ref):
    x = x_ref[...].reshape(x_ref.shape[-2:]).astype(BF16)
    dot = lambda w: jnp.dot(x, w, preferred_element_type=F32)
    q = dot(wqkv_ref[:, pl.ds(0, ATTN_W)]) * QSCALE
    kv = dot(wqkv_ref[:, pl.ds(ATTN_W, 2 * ATTN_W)])
    f = dot(wf_ref[...]) + bf_ref[...]
    lf = jnp.minimum(f, 0.0) - jnp.log(1.0 + jnp.exp(-jnp.abs(f)))
    glu = dot(wu_ref[...]) * (1.0 / (1.0 + jnp.exp(-dot(wg_ref[...]))))
    return q, kv[:, :ATTN_W], kv[:, ATTN_W:], lf, glu


def _store_values_with_ones(vta_ref, vt):
    ones = jnp.ones((BF16_ROWS, vt.shape[1]), BF16)
    for h in range(HEADS):
        vta_ref[0, pl.ds(h * V_ROWS, HEAD_DIM), :] = vt[h * HEAD_DIM:(h + 1) * HEAD_DIM].astype(BF16)
        vta_ref[0, pl.ds(h * V_ROWS + HEAD_DIM, BF16_ROWS), :] = ones


def _in_proj_kernel(x_ref, wqkv_ref, wf_ref, wu_ref, wg_ref, bf_ref, *out_refs, sample):
    q, k, v, lf, glu = _projections(x_ref, wqkv_ref, wf_ref, wu_ref, wg_ref, bf_ref)
    vt = v.T
    if sample:
        qt_ref, kt_ref, vt_ref, lf_ref, glu_ref = out_refs
        qt_ref[0] = q.T
    else:
        kb_ref, kt_ref, vt_ref, vta_ref, lf_ref, glu_ref = out_refs
        kb_ref[0] = k.astype(BF16)
        _store_values_with_ones(vta_ref, vt)
    kt_ref[0] = k.T
    vt_ref[0] = vt
    lf_ref[0] = lf
    glu_ref[0] = glu


def _prompt_proj_kernel(x_ref, wqkv_ref, wf_ref, wu_ref, wg_ref, bf_ref,
                        kmt_ref, vmt_ref, c0_ref, halo_ref, cw_ref, cb_ref, cg_ref, cbeta_ref,
                        qt_ref, kb_ref, ktf_ref, vtf_ref, vta_ref, ck_ref, lft_ref, glu_ref, conv_ref,
                        win_ref, sh_ref, kc_ref, vc_ref, cc_ref):
    j = pl.program_id(1)
    nt = pl.num_programs(1) - 1
    bm = x_ref.shape[1]

    @pl.when(j == 0)
    def _():
        kc_ref[...] = kmt_ref[...]
        vc_ref[...] = vmt_ref[...]
        cc_ref[...] = c0_ref[...]
        win_ref[pl.ds(0, CONV_HALO), :] = halo_ref[...]

    lane = lax.broadcasted_iota(jnp.int32, (ATTN_W, LANES), 1)

    @pl.when(j < nt)
    def _():
        q, k, v, lf, glu = _projections(x_ref, wqkv_ref, wf_ref, wu_ref, wg_ref, bf_ref)
        qt_ref[0] = q.T.astype(BF16)
        kb_ref[0] = k.astype(BF16)
        vt = v.T
        _store_values_with_ones(vta_ref, vt)
        for slab, out_ref, carry_ref in ((k.T, ktf_ref, kc_ref), (vt, vtf_ref, vc_ref)):
            rolled = pltpu.roll(slab, N_META, axis=1)
            out_ref[0, :, pl.ds(0, LANES)] = jnp.where(lane < N_META, carry_ref[...], rolled[:, :LANES])
            out_ref[0, :, pl.ds(LANES, bm - LANES)] = rolled[:, LANES:]
            carry_ref[...] = rolled[:, :LANES]
        carry = _cumsum_rows(lf, cc_ref[pl.ds(0, 1), :], ck_ref, lft_ref)
        cc_ref[...] = jnp.broadcast_to(carry, (SUBLANES, LANES))
        glu_ref[0] = glu

        win_ref[pl.ds(CONV_HALO, bm), :] = glu
        span = CONV_HALO + bm - SUBLANES
        for r in range(1, SUBLANES):
            sh_ref[r, pl.ds(0, span), :] = win_ref[pl.ds(r, span), :]
        base = CONV_HALO - TAIL
        groups = CONV_SUB // SUBLANES
        for s0 in range(0, bm, CONV_SUB):
            acc = None
            for tap in range(CONV_W):
                off = s0 + base + tap
                r, a = off % SUBLANES, off - off % SUBLANES
                src = win_ref[pl.ds(a, CONV_SUB), :] if r == 0 else sh_ref[r, pl.ds(a, CONV_SUB), :]
                t = cw_ref[tap][None] * src.reshape(groups, SUBLANES, CONV_CH)
                acc = t if acc is None else acc + t
            y = _layer_norm(acc.reshape(CONV_SUB, CONV_CH) + cb_ref[...], cg_ref[...], cbeta_ref[...])
            conv_ref[0, pl.ds(s0, CONV_SUB), :] = _silu(y).astype(conv_ref.dtype)
        win_ref[pl.ds(0, CONV_HALO), :] = win_ref[pl.ds(bm, CONV_HALO), :]

    @pl.when(j == nt)
    def _():
        for out_ref, carry_ref in ((ktf_ref, kc_ref), (vtf_ref, vc_ref)):
            out_ref[0] = jnp.zeros(out_ref.shape[1:], F32)
            out_ref[0, :, pl.ds(0, LANES)] = carry_ref[...]


def _weight_specs(ws):
    return [_const_spec(w.shape) for w in ws]


def _in_proj(x, ws, sample):
    _, t, d = x.shape
    nat = lambda w, dt: jax.ShapeDtypeStruct((1, t, w), dt)
    tr = lambda r, dt: jax.ShapeDtypeStruct((1, r, t), dt)
    if sample:
        outs = [tr(ATTN_W, F32), tr(ATTN_W, F32), tr(ATTN_W, F32), nat(LANES, F32), nat(CONV_CH, F32)]
    else:
        outs = [nat(ATTN_W, BF16), tr(ATTN_W, F32), tr(ATTN_W, F32), tr(HEADS * V_ROWS, BF16),
                nat(LANES, F32), nat(CONV_CH, F32)]
    return pl.pallas_call(
        functools.partial(_in_proj_kernel, sample=sample),
        out_shape=outs,
        compiler_params=pltpu.CompilerParams(vmem_limit_bytes=VMEM_LIMIT),
        name="in_proj",
    )(x, *ws)


def _prompt_proj(x, ws, kmt, vmt, c0, conv_args, bm):
    b, t, d = x.shape
    assert t % bm == 0 and bm > LANES and bm % CS_CHUNK == 0
    nt = t // bm
    cur = lambda j: jnp.minimum(j, nt - 1)
    row = lambda w: pl.BlockSpec((1, bm, w), lambda i, j: (i, cur(j), 0))
    col = lambda r: pl.BlockSpec((1, r, bm), lambda i, j: (i, 0, cur(j)))
    shifted = pl.BlockSpec((1, ATTN_W, bm), lambda i, j: (i, 0, j))
    nat = lambda w, dt: jax.ShapeDtypeStruct((b, t, w), dt)
    tr = lambda r, tt, dt: jax.ShapeDtypeStruct((b, r, tt), dt)
    outs = [tr(ATTN_W, t, BF16), nat(ATTN_W, BF16), tr(ATTN_W, N_META + t, F32), tr(ATTN_W, N_META + t, F32),
            tr(HEADS * V_ROWS, t, BF16), jax.ShapeDtypeStruct((b, HEADS, t, LANES), F32), tr(HEADS, t, F32),
            nat(CONV_CH, F32), nat(CONV_CH, BF16)]
    specs = [col(ATTN_W), row(ATTN_W), shifted, shifted, col(HEADS * V_ROWS),
             pl.BlockSpec((1, HEADS, bm, LANES), lambda i, j: (i, 0, cur(j), 0)), col(HEADS), row(CONV_CH),
             row(CONV_CH)]
    consts = list(ws) + [kmt, vmt, c0] + list(conv_args)
    return pl.pallas_call(
        _prompt_proj_kernel,
        grid=(b, nt + 1),
        in_specs=[row(d)] + _weight_specs(consts),
        out_specs=specs,
        out_shape=outs,
        scratch_shapes=[pltpu.VMEM((CONV_HALO + bm, CONV_CH), F32),
                        pltpu.VMEM((SUBLANES, CONV_HALO + bm, CONV_CH), F32),
                        pltpu.VMEM((ATTN_W, LANES), F32), pltpu.VMEM((ATTN_W, LANES), F32),
                        pltpu.VMEM((SUBLANES, LANES), F32)],
        compiler_params=pltpu.CompilerParams(dimension_semantics=("arbitrary", "arbitrary"),
                                             vmem_limit_bytes=VMEM_LIMIT),
        name="prompt_proj",
    )(x, *consts)


CS_CHUNK = 256


def _cumsum_rows(lf, carry, ck_ref, lft_ref):
    rr = lax.broadcasted_iota(jnp.int32, (CS_CHUNK, CS_CHUNK), 0)
    cc = lax.broadcasted_iota(jnp.int32, (CS_CHUNK, CS_CHUNK), 1)
    tri = (cc <= rr).astype(BF16)
    for i in range(lf.shape[0] // CS_CHUNK):
        sl = pl.ds(i * CS_CHUNK, CS_CHUNK)
        chunk = lf[i * CS_CHUNK:(i + 1) * CS_CHUNK]
        c = _dot_exact01(chunk, tri, lhs=False) + carry
        for h in range(HEADS):
            ck_ref[0, h, sl, :] = jnp.broadcast_to(c[:, h:h + 1] * LOG2E, (CS_CHUNK, LANES))
        lft_ref[0, :, sl] = jnp.transpose(chunk)[:HEADS]
        carry = c[CS_CHUNK - 1:CS_CHUNK, :]
    return carry


def _meta_cumsum_kernel(lfm_ref, cm_ref, carry_ref):
    r16 = lax.broadcasted_iota(jnp.int32, (N_META, N_META), 0)
    c16 = lax.broadcasted_iota(jnp.int32, (N_META, N_META), 1)
    tri16 = (c16 <= r16).astype(BF16)
    cm = _dot_exact01(lfm_ref[...], tri16, lhs=False)
    for h in range(HEADS):
        cm_ref[h] = jnp.broadcast_to(cm[:, h:h + 1] * LOG2E, (N_META, LANES))
    carry_ref[...] = jnp.broadcast_to(cm[N_META - 1:N_META, :], (SUBLANES, LANES))


def _meta_cumsum(lf_meta):
    return pl.pallas_call(
        _meta_cumsum_kernel,
        out_shape=[jax.ShapeDtypeStruct((HEADS, N_META, LANES), F32),
                   jax.ShapeDtypeStruct((SUBLANES, LANES), F32)],
        compiler_params=pltpu.CompilerParams(vmem_limit_bytes=VMEM_LIMIT),
        name="meta_cumsum",
    )(lf_meta)


TQ = 512
TK = 256


def _flash_kernel(qt_ref, kb_ref, vta_ref, ck_ref, km_ref, vmta_ref, cm_ref, o_ref, u_scr):
    qi = pl.program_id(2)
    qt = qt_ref[0].astype(F32)
    feat = lax.broadcasted_iota(jnp.int32, (LANES, TQ), 0)
    reps = TQ // LANES
    qts = [jnp.where((feat // HEAD_DIM) == hh, qt, 0.0).astype(BF16) for hh in range(2)]

    def scores(j):
        off = pl.multiple_of(j * TK, TK)
        kblk = kb_ref[0, pl.ds(off, TK), :]
        return [jnp.dot(kblk, qts[hh], preferred_element_type=F32) for hh in range(2)]

    def stage1(j, s):
        off = pl.multiple_of(j * TK, TK)
        mbs = []
        for hh in range(2):
            u = s[hh] - jnp.concatenate([ck_ref[0, hh, pl.ds(off, TK), :]] * reps, axis=1)
            u_scr[j % 2, hh] = u
            mbs.append(jnp.max(u, axis=0, keepdims=True))
        return mbs

    def stage2(u, m_new, vta):
        p = jnp.exp2(u - m_new).astype(BF16)
        return jnp.dot(vta, p, preferred_element_type=F32)

    def values(j, hh):
        off = pl.multiple_of(j * TK, TK)
        return vta_ref[0, pl.ds(hh * V_ROWS, V_ROWS), pl.ds(off, TK)]

    s0 = scores(0)
    s_meta = [jnp.dot(km_ref[...], qts[hh], preferred_element_type=F32) for hh in range(2)]
    mbs = stage1(0, s0)
    u_meta = [s_meta[hh] - jnp.concatenate([cm_ref[hh]] * reps, axis=1) for hh in range(2)]
    m_meta = [jnp.max(u, axis=0, keepdims=True) for u in u_meta]

    def body(j, carry):
        mbs, state = carry[:2], carry[2:]
        s_next = scores(j + 1)
        out = []
        for hh in range(2):
            m, acc = state[2 * hh], state[2 * hh + 1]
            m_new = jnp.maximum(m, mbs[hh])
            pv = stage2(u_scr[j % 2, hh], m_new, values(j, hh))
            out += [m_new, jnp.exp2(m - m_new) * acc + pv]
        return tuple(stage1(j + 1, s_next)) + tuple(out)

    n_diag = TQ // TK
    n_full = qi * n_diag
    empty = (jnp.full((1, TQ), NEG, F32), jnp.zeros((V_ROWS, TQ), F32))
    carry = lax.fori_loop(0, n_full, body, tuple(mbs) + empty + empty)
    state = list(carry[2:])

    s_diag = [scores(n_full + d) for d in range(1, n_diag)]
    pv_meta = [stage2(u_meta[hh], m_meta[hh], vmta_ref[pl.ds(hh * V_ROWS, V_ROWS), :]) for hh in range(2)]
    key = lax.broadcasted_iota(jnp.int32, (TK, TQ), 0)
    qry = lax.broadcasted_iota(jnp.int32, (TK, TQ), 1)
    for d in range(n_diag):
        off = pl.multiple_of((n_full + d) * TK, TK)
        for hh in range(2):
            m, acc = state[2 * hh], state[2 * hh + 1]
            if d == 0:
                u = u_scr[n_full % 2, hh]
            else:
                u = s_diag[d - 1][hh] - jnp.concatenate([ck_ref[0, hh, pl.ds(off, TK), :]] * reps, axis=1)
            u = jnp.where(key + d * TK <= qry, u, NEG)
            m_new = jnp.maximum(m, jnp.max(u, axis=0, keepdims=True))
            pv = stage2(u, m_new, values(n_full + d, hh))
            state[2 * hh], state[2 * hh + 1] = m_new, jnp.exp2(m - m_new) * acc + pv
    outs = []
    for hh in range(2):
        m, acc = state[2 * hh], state[2 * hh + 1]
        m_new = jnp.maximum(m, m_meta[hh])
        acc = jnp.exp2(m - m_new) * acc + jnp.exp2(m_meta[hh] - m_new) * pv_meta[hh]
        outs.append(acc[:HEAD_DIM] / acc[HEAD_DIM:HEAD_DIM + 1])
    o_ref[0] = jnp.concatenate(outs, axis=0).T.astype(o_ref.dtype)


def _prompt_attention(qt, kb, vta, ckrep, kmb, vmta, cmrep):
    b, _, t = qt.shape
    assert t % TQ == 0 and TQ % TK == 0
    grid = (b, ATTN_W // LANES, t // TQ)
    return pl.pallas_call(
        _flash_kernel,
        grid=grid,
        in_specs=[
            pl.BlockSpec((1, LANES, TQ), lambda i, p, j: (i, p, j)),
            pl.BlockSpec((1, t, LANES), lambda i, p, j: (i, 0, p)),
            pl.BlockSpec((1, 2 * V_ROWS, t), lambda i, p, j: (i, p, 0)),
            pl.BlockSpec((1, 2, t, LANES), lambda i, p, j: (i, p, 0, 0)),
            pl.BlockSpec((N_META, LANES), lambda i, p, j: (0, p)),
            pl.BlockSpec((2 * V_ROWS, N_META), lambda i, p, j: (p, 0)),
            pl.BlockSpec((2, N_META, LANES), lambda i, p, j: (p, 0, 0)),
        ],
        out_specs=pl.BlockSpec((1, TQ, LANES), lambda i, p, j: (i, j, p)),
        out_shape=jax.ShapeDtypeStruct((b, t, ATTN_W), BF16),
        scratch_shapes=[pltpu.VMEM((2, 2, TK, TQ), F32)],
        compiler_params=pltpu.CompilerParams(dimension_semantics=("arbitrary",) * 3,
                                             vmem_limit_bytes=VMEM_LIMIT),
        name="prompt_attention",
    )(qt, kb, vta, ckrep, kmb, vmta, cmrep)


def _sample_conv_kernel(st_ref, glu_ref, w_ref, cb_ref, g_ref, b_ref, o_ref):
    w = w_ref[...]
    acc = w[TAIL:TAIL + 1, :] * glu_ref[...]
    for j in range(TAIL):
        acc = acc + w[j:j + 1, :] * st_ref[j]
    y = _layer_norm(acc + cb_ref[...], g_ref[...], b_ref[...])
    o_ref[...] = _silu(y).astype(o_ref.dtype)


def _sample_conv(state_t, glus, conv_w, conv_b, g, b):
    n = glus.shape[0]
    return pl.pallas_call(
        _sample_conv_kernel,
        out_shape=jax.ShapeDtypeStruct((n, CONV_CH), BF16),
        compiler_params=pltpu.CompilerParams(vmem_limit_bytes=VMEM_LIMIT),
        name="sample_conv",
    )(state_t, glus, conv_w, conv_b, g, b)


FF_CHUNK = 1024


def _mlp_kernel(x_ref, a_ref, c_ref, woa_ref, woc_ref, g1_ref, b1_ref, wup_ref, wdn_ref, g2_ref, b2_ref,
                o_ref):
    mixed = jnp.dot(a_ref[...], woa_ref[...], preferred_element_type=F32)
    mixed = mixed + jnp.dot(c_ref[...], woc_ref[...], preferred_element_type=F32)
    h = _layer_norm(ALPHA * x_ref[...] + mixed, g1_ref[...], b1_ref[...])
    hb = h.astype(BF16)
    d_ff = wup_ref.shape[1]
    m = None
    for c0 in range(0, d_ff, FF_CHUNK):
        u = jnp.dot(hb, wup_ref[:, pl.ds(c0, FF_CHUNK)], preferred_element_type=F32)
        u = jnp.maximum(u, 0.0)
        u = (u * u).astype(BF16)
        t = jnp.dot(u, wdn_ref[pl.ds(c0, FF_CHUNK), :], preferred_element_type=F32)
        m = t if m is None else m + t
    o_ref[...] = _layer_norm(ALPHA * h + m, g2_ref[...], b2_ref[...])


def _merge_mlp(x, attn, conv, woa, woc, g1, b1, wup, wdn, g2, b2, bm):
    m, d = x.shape
    assert m % bm == 0
    row = lambda w: pl.BlockSpec((bm, w), lambda i: (i, 0))
    consts = [woa, woc, g1, b1, wup, wdn, g2, b2]
    return pl.pallas_call(
        _mlp_kernel,
        grid=(m // bm,),
        in_specs=[row(d), row(ATTN_W), row(CONV_CH)] + [_const_spec(a.shape) for a in consts],
        out_specs=row(d),
        out_shape=jax.ShapeDtypeStruct((m, d), F32),
        compiler_params=pltpu.CompilerParams(dimension_semantics=("arbitrary",),
                                             vmem_limit_bytes=VMEM_LIMIT),
        name="merge_mlp",
    )(x, attn, conv, *consts)


NBUF = 32
PGROUP = 4


def _per_head(col):
    return jnp.broadcast_to(col[:, None, :], (HEADS, HEAD_DIM, 1)).reshape(ATTN_W, 1)


def _paged_attn_kernel(pt_ref, qt_ref, knt_ref, vnt_ref, lfn_ref, lft_hbm, ck_hbm, cv_hbm, o_ref,
                       buf, sem, lfbuf, lfsem, d_scr, s_all, stat_ref, qb_ref, acc_ref, ctr, first_ref):
    t = pl.program_id(0)
    n = first_ref.shape[0]
    n_pages = s_all.shape[1]
    nbuf = buf.shape[0]
    rows = n_pages * HEADS
    ISSUED, CONSUMED, I_TRIP, I_VALUES, I_OFF = range(5)

    def issue_next():
        it, in_v, off = ctr[I_TRIP], ctr[I_VALUES], ctr[I_OFF]
        slot = ctr[ISSUED] % nbuf

        @pl.when((it <= n) & (in_v == 0))
        def _():
            pltpu.make_async_copy(ck_hbm.at[pt_ref[it * n_pages + off]], buf.at[slot], sem.at[slot]).start()

        @pl.when((it <= n) & (in_v == 1))
        def _():
            pltpu.make_async_copy(cv_hbm.at[pt_ref[(it - 1) * n_pages + off]], buf.at[slot],
                                  sem.at[slot]).start()

        ctr[ISSUED] = ctr[ISSUED] + 1
        seg_done = off + 1 == n_pages
        to_values = seg_done & (in_v == 0) & (it >= 1)
        to_next = seg_done & ((in_v == 1) | (it == 0))
        nxt = it + 1
        nxt_values = nxt >= n
        ctr[I_TRIP] = jnp.where(to_next, nxt, it)
        ctr[I_VALUES] = jnp.where(to_values, 1, jnp.where(to_next, nxt_values.astype(jnp.int32), in_v))
        first_cur = first_ref[jnp.clip(it - 1, 0, n - 1)]
        first_nxt = first_ref[jnp.minimum(it, n - 1)]
        ctr[I_OFF] = jnp.where(to_values, first_cur,
                               jnp.where(to_next, jnp.where(nxt_values, first_nxt, 0), off + 1))

    def consume():
        slot = ctr[CONSUMED] % nbuf
        pltpu.make_async_copy(ck_hbm.at[0], buf.at[slot], sem.at[slot]).wait()
        ctr[CONSUMED] = ctr[CONSUMED] + 1
        return slot

    def lf_copy(seq, slot, p):
        return pltpu.make_async_copy(lft_hbm.at[pt_ref[seq * n_pages + p]], lfbuf.at[slot, p], lfsem.at[slot])

    def lf_start(seq, slot):
        def body(p, _):
            lf_copy(seq, slot, p).start()
            return 0
        lax.fori_loop(0, n_pages, body, 0)

    @pl.when(t == 0)
    def _():
        for i in range(5):
            ctr[i] = 0

        def clear(i, _):
            first_ref[i] = 0
            return 0
        lax.fori_loop(0, n, clear, 0)
        for _ in range(nbuf):
            issue_next()
        lf_start(0, 0)

    @pl.when(t + 1 < n)
    def _():
        lf_start(t + 1, (t + 1) % 2)

    seq_lane = lax.broadcasted_iota(jnp.int32, qt_ref.shape, 1)
    pick = lambda ref, seq: jnp.sum(jnp.where(seq_lane == seq, ref[...], 0.0), axis=1, keepdims=True)

    @pl.when(t < n)
    def _():
        _paged_keys(t, n_pages, rows, pick, consume, issue_next, lf_copy,
                    qt_ref, knt_ref, lfn_ref, buf, lfbuf, d_scr, s_all, stat_ref, qb_ref, first_ref)

    @pl.when(t >= 1)
    def _():
        _paged_values(t - 1, n_pages, pick, consume, issue_next, vnt_ref, buf, s_all, stat_ref, acc_ref,
                      first_ref, o_ref)


def _paged_keys(b, n_pages, rows, pick, consume, issue_next, lf_copy,
                qt_ref, knt_ref, lfn_ref, buf, lfbuf, d_scr, s_all, stat_ref, qb_ref, first_ref):
    par = b % 2

    def lf_wait(p, _):
        lf_copy(b, par, p).wait()
        return 0
    lax.fori_loop(0, n_pages, lf_wait, 0)

    lf = lfbuf[par].reshape(rows, PAGE)
    s_in = lax.broadcasted_iota(jnp.int32, (PAGE, PAGE), 0)
    s_out = lax.broadcasted_iota(jnp.int32, (PAGE, PAGE), 1)
    later = (s_in > s_out).astype(BF16)
    local = _dot_exact01(lf, later, lhs=True)
    tot = _dot_exact01(lf, jnp.ones((PAGE, PAGE), BF16), lhs=True)
    later_pages = lfn_ref[0]
    for p in reversed(range(n_pages)):
        sl = pl.ds(p * HEADS, HEADS)
        d_scr[sl, :] = (local[p * HEADS:(p + 1) * HEADS] + later_pages) * LOG2E
        later_pages = later_pages + tot[p * HEADS:(p + 1) * HEADS]

    qcol, kcol = pick(qt_ref, b), pick(knt_ref, b)
    qb_ref[...] = jnp.broadcast_to(qcol, (ATTN_W, PAGE))
    s_new = jnp.sum((qcol * kcol).reshape(HEADS, HEAD_DIM, 1), axis=1)

    def key_group(g, _):
        for i in range(PGROUP):
            u = g * PGROUP + i
            slot = consume()
            prod = buf[slot] * qb_ref[...]
            s = jnp.sum(prod.reshape(HEADS, HEAD_DIM, PAGE), axis=1)
            s_all[par, u] = s + d_scr[pl.ds(pl.multiple_of(u * HEADS, HEADS), HEADS), :]
            issue_next()
        return 0

    lax.fori_loop(0, n_pages // PGROUP, key_group, 0)

    s = s_all[par]
    m = jnp.max(jnp.max(s, axis=0), axis=1, keepdims=True)
    m = jnp.maximum(m, s_new)
    p = jnp.exp2(s - m[None])
    p_new = jnp.exp2(s_new - m)
    l = jnp.sum(jnp.sum(p, axis=0), axis=1, keepdims=True) + p_new
    s_all[par] = p
    stat_ref[par, 0] = jnp.broadcast_to(l, (HEADS, PAGE))
    stat_ref[par, 1] = jnp.broadcast_to(p_new, (HEADS, PAGE))

    page_max = jnp.max(jnp.max(p, axis=1), axis=1, keepdims=True)
    page_idx = lax.broadcasted_iota(jnp.int32, (n_pages, 1), 0)
    first = jnp.min(jnp.where(page_max > 0.0, page_idx, n_pages - 1))
    last = b == first_ref.shape[0] - 1
    first_ref[b] = jnp.where(last, 0, (first // PGROUP) * PGROUP)


def _paged_values(b, n_pages, pick, consume, issue_next, vnt_ref, buf, s_all, stat_ref, acc_ref, first_ref,
                  o_ref):
    par = b % 2
    vcol = pick(vnt_ref, b)
    l = stat_ref[par, 0][:, :1]
    p_new = stat_ref[par, 1][:, :1]
    pos = lax.broadcasted_iota(jnp.int32, (ATTN_W, PAGE), 1)
    acc_ref[...] = jnp.where(pos == 0, _per_head(p_new) * vcol, 0.0)

    def value_group(g, _):
        part = None
        for i in range(PGROUP):
            u = g * PGROUP + i
            slot = consume()
            pe = jnp.broadcast_to(s_all[par, u][:, None, :], (HEADS, HEAD_DIM, PAGE)).reshape(ATTN_W, PAGE)
            t = buf[slot] * pe
            part = t if part is None else part + t
            issue_next()
        acc_ref[...] += part
        return 0

    lax.fori_loop(first_ref[b] // PGROUP, n_pages // PGROUP, value_group, 0)
    ocol = jnp.sum(acc_ref[...], axis=1, keepdims=True) / _per_head(l)

    @pl.when(b == 0)
    def _():
        o_ref[...] = jnp.zeros_like(o_ref)

    seq_lane = lax.broadcasted_iota(jnp.int32, o_ref.shape, 1)
    o_ref[...] = jnp.where(seq_lane == b, ocol, o_ref[...])


def _paged_attention(pt_flat, qt, knt, vnt, lfn_rep, lft_pool, cache_kt, cache_vt, n_pages):
    n = qt.shape[1]
    nbuf = min(NBUF, n_pages // 2)
    assert n_pages % PGROUP == 0 and nbuf >= 1
    rows = n_pages * HEADS
    full = pl.BlockSpec((ATTN_W, n), lambda i, pt: (0, 0))
    hbm = pl.BlockSpec(memory_space=pl.ANY)
    grid_spec = pltpu.PrefetchScalarGridSpec(
        num_scalar_prefetch=1,
        grid=(n + 1,),
        in_specs=[full, full, full,
                  pl.BlockSpec((1, HEADS, PAGE), lambda i, pt: (jnp.minimum(i, n - 1), 0, 0)), hbm, hbm, hbm],
        out_specs=full,
        scratch_shapes=[pltpu.VMEM((nbuf, ATTN_W, PAGE), F32), pltpu.SemaphoreType.DMA((nbuf,)),
                        pltpu.VMEM((2, n_pages, HEADS, PAGE), F32), pltpu.SemaphoreType.DMA((2,)),
                        pltpu.VMEM((rows, PAGE), F32),
                        pltpu.VMEM((2, n_pages, HEADS, PAGE), F32),
                        pltpu.VMEM((2, 2, HEADS, PAGE), F32),
                        pltpu.VMEM((ATTN_W, PAGE), F32), pltpu.VMEM((ATTN_W, PAGE), F32),
                        pltpu.SMEM((8,), jnp.int32), pltpu.SMEM((n,), jnp.int32)],
    )
    return pl.pallas_call(
        _paged_attn_kernel,
        grid_spec=grid_spec,
        out_shape=jax.ShapeDtypeStruct((ATTN_W, n), F32),
        compiler_params=pltpu.CompilerParams(dimension_semantics=("arbitrary",),
                                             vmem_limit_bytes=VMEM_LIMIT),
        name="paged_attention",
    )(pt_flat, qt, knt, vnt, lfn_rep, lft_pool, cache_kt, cache_vt)


def _pick_bm(m, target):
    bm = min(m, target)
    while m % bm:
        bm //= 2
    return bm


def _positions_minor(x):
    nd = x.ndim
    return jnp.transpose(x, (0, 1, nd - 1) + tuple(range(2, nd - 1)))


def kernel(x_prompt, x_sample, cache_k, cache_v, cache_logf, state_conv, page_table, meta_tokens, w_in, b_f,
           conv_w, conv_b, conv_ln_g, conv_ln_b, w_out, ln1_g, ln1_b, w_up, w_down, ln2_g, ln2_b):
    bsz, seq, d = x_prompt.shape
    dbsz, dseq, _ = x_sample.shape
    depth = w_in.shape[0]
    assert depth == 1 and dseq == 1
    n_pool = cache_k.shape[1]
    n_pages = page_table.shape[1]

    w = w_in[0]
    o_f = 3 * ATTN_W
    wqkv = w[:, :o_f].astype(BF16)
    wf = jnp.pad(w[:, o_f:o_f + HEADS], ((0, 0), (0, LANES - HEADS))).astype(BF16)
    wu = w[:, o_f + HEADS:o_f + HEADS + CONV_CH].astype(BF16)
    wg = w[:, o_f + HEADS + CONV_CH:].astype(BF16)
    bf = jnp.pad(b_f[0], (0, LANES - HEADS)).reshape(1, LANES)
    woa = w_out[0, :ATTN_W].astype(BF16)
    woc = w_out[0, ATTN_W:].astype(BF16)
    wup = w_up[0].astype(BF16)
    wdn = w_down[0].astype(BF16)
    row = lambda a: a[0].reshape(1, -1)
    cw, cb, cg, cbeta = conv_w[0], row(conv_b), row(conv_ln_g), row(conv_ln_b)
    cw_rep = jnp.broadcast_to(cw[:, None, :], (CONV_W, SUBLANES, CONV_CH))
    g1, b1, g2, b2 = row(ln1_g), row(ln1_b), row(ln2_g), row(ln2_b)
    ws = (wqkv, wf, wu, wg, bf)

    meta_pad = jnp.pad(meta_tokens, ((0, LANES - N_META), (0, 0)))[None]
    kmb, kmt, vmt, vmta, lfm, glum = _in_proj(meta_pad, ws, sample=False)
    lfm = lfm[0, :N_META]
    meta_halo = jnp.concatenate([jnp.zeros((CONV_HALO - N_META, CONV_CH), F32), glum[0, :N_META]], axis=0)
    cmrep, c0 = _meta_cumsum(lfm)
    qt, kb, ktf, vtf, vta, ckrep, lft, glu, conv = _prompt_proj(
        x_prompt, ws, kmt[0], vmt[0], c0, (meta_halo, cw_rep, cb, cg, cbeta), _pick_bm(seq, 512))
    attn = _prompt_attention(qt, kb, vta, ckrep, kmb[0, :N_META], vmta[0, :, :N_META], cmrep)
    y_prompt = _merge_mlp(x_prompt.reshape(bsz * seq, d), attn.reshape(bsz * seq, ATTN_W),
                          conv.reshape(bsz * seq, CONV_CH), woa, woc, g1, b1, wup, wdn, g2, b2,
                          _pick_bm(bsz * seq, 512))
    y_prompt = y_prompt.reshape(bsz, seq, d)

    k_prompt = _positions_minor(ktf.reshape(1, bsz, HEADS, HEAD_DIM, N_META + seq))
    v_prompt = _positions_minor(vtf.reshape(1, bsz, HEADS, HEAD_DIM, N_META + seq))
    lfm_t = jnp.broadcast_to(lfm.T[None, :HEADS], (bsz, HEADS, N_META))
    logf_prompt = _positions_minor(jnp.concatenate([lfm_t, lft], axis=2).reshape(1, bsz, HEADS, N_META + seq))
    conv_prompt = glu[:, seq - TAIL:, :].reshape(1, bsz, TAIL, CONV_CH)

    qst, kst, vst, lfs, glus = _in_proj(x_sample.reshape(1, dbsz, d), ws, sample=True)
    lfs, glus = lfs[0], glus[0]
    pt_flat = page_table.reshape(-1)
    cache_kt = jnp.transpose(cache_k, (0, 1, 3, 4, 2)).reshape(n_pool, ATTN_W, PAGE)
    cache_vt = jnp.transpose(cache_v, (0, 1, 3, 4, 2)).reshape(n_pool, ATTN_W, PAGE)
    lft_pool = jnp.transpose(cache_logf, (0, 1, 3, 2)).reshape(n_pool, HEADS, PAGE)
    lfn_rep = jnp.broadcast_to(lfs[:, :HEADS, None], (dbsz, HEADS, PAGE))
    attn_st = _paged_attention(pt_flat, qst[0], kst[0], vst[0], lfn_rep, lft_pool, cache_kt, cache_vt, n_pages)
    state_t = jnp.transpose(state_conv[0], (1, 0, 2))
    conv_s = _sample_conv(state_t, glus, cw, cb, cg, cbeta)
    y_sample = _merge_mlp(x_sample.reshape(dbsz, d), attn_st.T.astype(BF16), conv_s,
                          woa, woc, g1, b1, wup, wdn, g2, b2, dbsz)
    y_sample = y_sample.reshape(dbsz, 1, d)

    k_sample = jnp.transpose(kst.reshape(1, 1, HEADS, HEAD_DIM, dbsz), (0, 4, 1, 2, 3))
    v_sample = jnp.transpose(vst.reshape(1, 1, HEADS, HEAD_DIM, dbsz), (0, 4, 1, 2, 3))
    logf_sample = lfs[:, :HEADS].reshape(1, dbsz, 1, HEADS)
    conv_sample = jnp.transpose(jnp.concatenate([state_t[1:], glus[None]], axis=0), (1, 0, 2))[None]

    return (y_prompt, y_sample, k_prompt, v_prompt, logf_prompt, conv_prompt,
            k_sample, v_sample, logf_sample, conv_sample)
```
